```python
import math
import jax, jax.numpy as jnp
from jax import lax
import numpy as np

D_MODEL = 1024
BATCH = 4
SEQ = 8192
DEPTH = 2

CHUNK = 64
Q_BLOCK = 128
EPS = 1e-6

A_HEAD_DIM = 64
A_V_DIM = 2 * A_HEAD_DIM
A_WIDTH = D_MODEL // 2
A_HEADS = A_WIDTH // A_V_DIM

B_CHUNK = 128
B_GROUP_DIM = 128
B_WIDTH = D_MODEL - A_WIDTH
B_GROUPS = B_WIDTH // B_GROUP_DIM

EVEN_WIDTH = A_WIDTH + B_WIDTH
EVEN_IN = 4 * A_WIDTH + 3 * B_WIDTH

C_HEAD_DIM = 64
C_WIDTH = D_MODEL
C_HEADS = C_WIDTH // C_HEAD_DIM
ODD_IN = 4 * C_WIDTH

N_EVEN = (DEPTH + 1) // 2
N_ODD = DEPTH // 2

kernel_name = "hybrid_diffattn_sgu_stickbreaking"


def rms_norm(x, g):
    xf = x.astype(jnp.float32)
    y = xf * lax.rsqrt(jnp.mean(xf * xf, axis=-1, keepdims=True) + EPS)
    return (y * g.astype(jnp.float32)).astype(x.dtype)


def alibi_slopes(n_heads):
    return jnp.exp2(-8.0 * jnp.arange(1, n_heads + 1, dtype=jnp.float32) / n_heads)


def lambda_init_for(layer_idx):
    return 0.8 - 0.6 * math.exp(-0.3 * layer_idx)


def diff_attention(q, k, v, q_g, k_g, lam, lam_init, subln_g):
    bsz, seq, n_heads = q.shape[0], q.shape[1], q.shape[2]
    q = rms_norm(q, q_g).astype(jnp.float32)
    k = rms_norm(k, k_g).astype(jnp.float32)
    vf = v.astype(jnp.float32)
    lamf = lam.astype(jnp.float32)
    lam_full = (jnp.exp(jnp.sum(lamf[0] * lamf[1])) - jnp.exp(jnp.sum(lamf[2] * lamf[3]))
                + lam_init)
    slopes = alibi_slopes(n_heads)
    scale = A_HEAD_DIM ** -0.5
    outs = []
    for i in range(seq // Q_BLOCK):
        q0 = i * Q_BLOCK
        kend = q0 + Q_BLOCK
        qb = q[:, q0:kend]
        kb = k[:, :kend]
        s = jnp.einsum('bqhcd,bkhcd->bhcqk', qb, kb) * scale
        tq = jnp.arange(q0, kend)
        tk = jnp.arange(kend)
        dist = jnp.abs(tq[:, None] - tk[None, :]).astype(jnp.float32)
        s = s - slopes[None, :, None, None, None] * dist
        allowed = (tk[None, :] // CHUNK) <= (tq[:, None] // CHUNK)
        s = jnp.where(allowed, s, -jnp.inf)
        p = jax.nn.softmax(s, axis=-1)
        w = p[:, :, 0] - lam_full * p[:, :, 1]
        outs.append(jnp.einsum('bhqk,bkhe->bqhe', w, vf[:, :kend]))
    o = jnp.concatenate(outs, axis=1)
    o = rms_norm(o, subln_g) * (1.0 - lam_init)
    return o.reshape(bsz, seq, n_heads * A_V_DIM).astype(v.dtype)


def chunked_sgu(u, v, g_norm, w_s, b_s):
    bsz, seq = u.shape[0], u.shape[1]
    u = jax.nn.gelu(u)
    v = rms_norm(jax.nn.gelu(v), g_norm)
    vb = v.reshape(bsz, seq // B_CHUNK, B_CHUNK, B_GROUPS, B_GROUP_DIM)
    pos = jnp.arange(B_CHUNK)
    mask = (pos[:, None] // CHUNK) >= (pos[None, :] // CHUNK)
    ws = jnp.where(mask[None], w_s, 0.0).astype(v.dtype)
    mixed = jnp.einsum('gts,bnsgc->bntgc', ws, vb) + b_s.T[None, None, :, :, None]
    return u * mixed.reshape(bsz, seq, B_WIDTH)


def stick_breaking_attention(q, k, v):
    bsz, seq, n_heads, hd = q.shape
    qf = q.astype(jnp.float32)
    kf = k.astype(jnp.float32)
    vf = v.astype(jnp.float32)
    scale = hd ** -0.5
    outs = []
    for i in range(seq // Q_BLOCK):
        q0 = i * Q_BLOCK
        kend = q0 + Q_BLOCK
        z = jnp.einsum('bqhd,bkhd->bhqk', qf[:, q0:kend], kf[:, :kend]) * scale
        tq = jnp.arange(q0, kend)
        tk = jnp.arange(kend)
        strict = tk[None, :] < tq[:, None]
        log_beta = jax.nn.log_sigmoid(z)
        log_1mb = jnp.where(strict, jax.nn.log_sigmoid(-z), 0.0)
        tail = lax.cumsum(log_1mb, axis=3, reverse=True) - log_1mb
        a = jnp.where(strict, jnp.exp(log_beta + tail), 0.0)
        outs.append(jnp.einsum('bhqk,bkhd->bqhd', a, vf[:, :kend]))
    o = jnp.concatenate(outs, axis=1)
    return o.reshape(bsz, seq, n_heads * hd).astype(v.dtype)


def setup_inputs(seed: int = 0) -> dict:
    key = jax.random.key(seed)
    ks = jax.random.split(key, 16)
    f32 = jnp.float32
    x = jax.random.normal(ks[0], (BATCH, SEQ, D_MODEL), f32)
    norm_g = 1.0 + 0.02 * jax.random.normal(ks[1], (DEPTH, D_MODEL), f32)
    w_in_even = jax.random.normal(ks[2], (N_EVEN, D_MODEL, EVEN_IN), f32) * D_MODEL ** -0.5
    q_norm_g = 1.0 + 0.02 * jax.random.normal(ks[3], (N_EVEN, A_HEAD_DIM), f32)
    k_norm_g = 1.0 + 0.02 * jax.random.normal(ks[4], (N_EVEN, A_HEAD_DIM), f32)
    lam = 0.1 * jax.random.normal(ks[5], (N_EVEN, 4, A_HEAD_DIM), f32)
    subln_g = 1.0 + 0.02 * jax.random.normal(ks[6], (N_EVEN, A_V_DIM), f32)
    sgu_norm_g = 1.0 + 0.02 * jax.random.normal(ks[7], (N_EVEN, B_WIDTH), f32)
    w_s = jax.random.normal(ks[8], (N_EVEN, B_GROUPS, B_CHUNK, B_CHUNK), f32) * B_CHUNK ** -0.5
    b_s = 1.0 + 0.01 * jax.random.normal(ks[9], (N_EVEN, B_GROUPS, B_CHUNK), f32)
    w_out_even = jax.random.normal(ks[10], (N_EVEN, EVEN_WIDTH, D_MODEL), f32) * EVEN_WIDTH ** -0.5
    w_in_odd = jax.random.normal(ks[11], (N_ODD, D_MODEL, ODD_IN), f32) * D_MODEL ** -0.5
    w_out_odd = jax.random.normal(ks[12], (N_ODD, C_WIDTH, D_MODEL), f32) * C_WIDTH ** -0.5
    return {"x": x, "norm_g": norm_g, "w_in_even": w_in_even, "q_norm_g": q_norm_g,
            "k_norm_g": k_norm_g, "lam": lam, "subln_g": subln_g, "sgu_norm_g": sgu_norm_g,
            "w_s": w_s, "b_s": b_s, "w_out_even": w_out_even, "w_in_odd": w_in_odd,
            "w_out_odd": w_out_odd}


def reference(x, norm_g, w_in_even, q_norm_g, k_norm_g, lam, subln_g, sgu_norm_g,
              w_s, b_s, w_out_even, w_in_odd, w_out_odd):
    bsz, seq = x.shape[0], x.shape[1]
    for l in range(DEPTH):
        h = rms_norm(x, norm_g[l])
        if l % 2 == 0:
            e = l // 2
            proj = h @ w_in_even[e]
            qa, ka, va, ga, ub, vb, gb = jnp.split(
                proj, np.cumsum([A_WIDTH, A_WIDTH, A_WIDTH, A_WIDTH, B_WIDTH, B_WIDTH]).tolist(),
                axis=-1)
            qa = qa.reshape(bsz, seq, A_HEADS, 2, A_HEAD_DIM)
            ka = ka.reshape(bsz, seq, A_HEADS, 2, A_HEAD_DIM)
            va = va.reshape(bsz, seq, A_HEADS, A_V_DIM)
            oa = diff_attention(qa, ka, va, q_norm_g[e], k_norm_g[e], lam[e],
                                lambda_init_for(l), subln_g[e])
            ob = chunked_sgu(ub, vb, sgu_norm_g[e], w_s[e], b_s[e])
            mixed = jnp.concatenate([oa * jax.nn.silu(ga), ob * jax.nn.silu(gb)], axis=-1)
            x = x + mixed @ w_out_even[e]
        else:
            o = l // 2
            proj = h @ w_in_odd[o]
            qc, kc, vc, gc = jnp.split(proj, 4, axis=-1)
            qc = qc.reshape(bsz, seq, C_HEADS, C_HEAD_DIM)
            kc = kc.reshape(bsz, seq, C_HEADS, C_HEAD_DIM)
            vc = vc.reshape(bsz, seq, C_HEADS, C_HEAD_DIM)
            oc = stick_breaking_attention(qc, kc, vc)
            x = x + (oc * jax.nn.silu(gc)) @ w_out_odd[o]
    return x
```

```python
import functools
import math

import jax
import jax.numpy as jnp
from jax import lax
from jax.experimental import pallas as pl
from jax.experimental.pallas import tpu as pltpu

EPS = 1e-6
LOG2E = 1.4426950408889634
CHUNK = 64
HEAD_DIM = 64
LANES = 128
A_HEADS = 4
A_WIDTH = 512
B_WIDTH = 512
B_CHUNK = 128
B_GROUPS = 4
C_WIDTH = 1024
LAMBDA_INIT_0 = 0.8 - 0.6 * math.exp(-0.3 * 0)
NEG_BIG = -1e30
VMEM_LIMIT_BYTES = 56 * 1024 * 1024

F32 = jnp.float32
BF16 = jnp.bfloat16


def _tile_sizes(seq):
    blk = 256 if seq % 256 == 0 else 128
    tm = 512 if seq % 512 == 0 else blk
    return tm, blk


def _dot(a, b):
    return jnp.dot(a, b, preferred_element_type=F32)


def _dot_nt(a, b):
    return lax.dot_general(a, b, (((1,), (1,)), ((), ())), preferred_element_type=F32)


def _silu(t):
    return t * (1.0 / (1.0 + jnp.exp(-t)))


def _rms(t, gain):
    ms = jnp.mean(t * t, axis=-1, keepdims=True)
    return t * lax.rsqrt(ms + EPS) * gain


def _even_in_kernel(x_ref, g_ref, w_ref, qg_ref, kg_ref, gmat_ref, sg_ref, ws_ref, bs_ref,
                    q_ref, k_ref, vt_ref, ga_ref, mb_ref, *, tm, blk):
    h = _rms(x_ref[0], g_ref[...]).astype(BF16)

    def proj(c0, width):
        return _dot(h, w_ref[:, c0:c0 + width])

    def head_norm(t, gain_ref):
        sq = t * t
        hi = sq.astype(BF16)
        lo = (sq - hi.astype(F32)).astype(BF16)
        ms = _dot(hi, gmat_ref[...]) + _dot(lo, gmat_ref[...])
        return t * lax.rsqrt(ms + EPS) * gain_ref[...]

    q_ref[0] = head_norm(proj(0, A_WIDTH), qg_ref).astype(BF16)
    k_ref[0] = head_norm(proj(A_WIDTH, A_WIDTH), kg_ref).astype(BF16)
    va = proj(2 * A_WIDTH, A_WIDTH)
    for j in range(tm // blk):
        vt_ref[0, j] = va[j * blk:(j + 1) * blk, :].T.astype(BF16)
    ga_ref[0] = _silu(proj(3 * A_WIDTH, A_WIDTH)).astype(BF16)

    base = 4 * A_WIDTH
    u = jax.nn.gelu(proj(base, B_WIDTH))
    vn = _rms(jax.nn.gelu(proj(base + B_WIDTH, B_WIDTH)), sg_ref[...]).astype(BF16)
    sgate = _silu(proj(base + 2 * B_WIDTH, B_WIDTH))
    gd = B_WIDTH // B_GROUPS
    for c in range(tm // B_CHUNK):
        rows = slice(c * B_CHUNK, (c + 1) * B_CHUNK)
        for g in range(B_GROUPS):
            cols = slice(g * gd, (g + 1) * gd)
            mixed = _dot(ws_ref[g], vn[rows, cols]) + bs_ref[:, cols]
            mb_ref[0, rows, cols] = (u[rows, cols] * mixed * sgate[rows, cols]).astype(BF16)


def _diff_attn_kernel(sl_ref, q_ref, k_ref, vt_ref, gate_ref, lam_ref, sub_ref, o_ref, acc_ref, *, blk):
    h = pl.program_id(1)
    i = pl.program_id(2)
    slope2 = sl_ref[h]

    q = q_ref[0]
    lane = lax.broadcasted_iota(jnp.int32, q.shape, 1)
    zero = jnp.zeros_like(q)
    qz = jnp.concatenate([jnp.where(lane < HEAD_DIM, q, zero),
                          jnp.where(lane >= HEAD_DIM, q, zero)], axis=0)

    shape = (blk, 2 * blk)
    kl = lax.broadcasted_iota(jnp.int32, shape, 0)
    ql = lax.broadcasted_iota(jnp.int32, shape, 1) & (blk - 1)

    raw = _dot_nt(k_ref[0, i], qz)
    bias = slope2 * (ql - jnp.abs(kl - ql)).astype(F32)
    e = jnp.where((kl // CHUNK) <= (ql // CHUNK), raw + bias, NEG_BIG)
    m0 = jnp.max(e, axis=0, keepdims=True)
    p = jnp.exp2(e - m0)
    l0 = jnp.sum(p, axis=0, keepdims=True)
    acc_ref[...] = _dot(vt_ref[0, i], p.astype(BF16))

    kbias = slope2 * kl.astype(F32)

    def body(kb, carry):
        m, l = carry
        c = slope2 * ((kb - i) * blk).astype(F32)
        e1 = _dot_nt(k_ref[0, kb], qz) + kbias
        m_new = jnp.maximum(m, jnp.max(e1, axis=0, keepdims=True) + c)
        alpha = jnp.exp2(m - m_new)
        p1 = jnp.exp2(e1 - (m_new - c))
        l_new = alpha * l + jnp.sum(p1, axis=0, keepdims=True)
        acc_ref[...] = acc_ref[...] * alpha + _dot(vt_ref[0, kb], p1.astype(BF16))
        return m_new, l_new

    _, l = lax.fori_loop(0, i, body, (m0, l0))

    lam = lam_ref[...]
    lam_full = (jnp.exp(jnp.sum(lam[0:1] * lam[1:2], axis=1, keepdims=True))
                - jnp.exp(jnp.sum(lam[2:3] * lam[3:4], axis=1, keepdims=True)) + LAMBDA_INIT_0)
    acc = acc_ref[...] * (1.0 / l)
    o = (acc[:, :blk] - lam_full * acc[:, blk:]).T
    o = _rms(o, sub_ref[...]) * (1.0 - LAMBDA_INIT_0)
    o_ref[0] = (o * gate_ref[0].astype(F32)).astype(BF16)


def _mid_kernel(x_ref, ma_ref, mb_ref, wo_ref, g_ref, wi_ref,
                x1_ref, q_ref, k_ref, vt_ref, gc_ref, *, tm, blk):
    x1 = x_ref[0] + _dot(ma_ref[0], wo_ref[0:A_WIDTH, :]) + _dot(mb_ref[0], wo_ref[A_WIDTH:, :])
    x1_ref[0] = x1
    h = _rms(x1, g_ref[...]).astype(BF16)

    def proj(c0):
        return _dot(h, wi_ref[:, c0:c0 + C_WIDTH])

    q_ref[0] = (proj(0) * (HEAD_DIM ** -0.5 * LOG2E)).astype(BF16)
    k_ref[0] = proj(C_WIDTH).astype(BF16)
    v = proj(2 * C_WIDTH)
    for j in range(tm // blk):
        vt_ref[0, j] = v[j * blk:(j + 1) * blk, :].T.astype(BF16)
    gc_ref[0] = _silu(proj(3 * C_WIDTH)).astype(BF16)


def _stick_kernel(q_ref, k_ref, vt_ref, gate_ref, tri_ref, o_ref, acc_ref, *, blk):
    i = pl.program_id(2)
    q = q_ref[0]
    lane = lax.broadcasted_iota(jnp.int32, q.shape, 1)
    zero = jnp.zeros_like(q)
    qz = jnp.concatenate([jnp.where(lane < HEAD_DIM, q, zero),
                          jnp.where(lane >= HEAD_DIM, q, zero)], axis=0)

    def log2_terms(z):
        sp = jnp.log(1.0 + jnp.exp2(-jnp.abs(z))) * LOG2E
        lb = jnp.minimum(z, 0.0) - sp
        return lb, lb - z

    shape = (blk, 2 * blk)
    kl = lax.broadcasted_iota(jnp.int32, shape, 0)
    ql = lax.broadcasted_iota(jnp.int32, shape, 1) & (blk - 1)
    strict = kl < ql

    lb, l1 = log2_terms(_dot_nt(k_ref[0, i], qz))
    l1 = jnp.where(strict, l1, 0.0)
    tail = _dot(tri_ref[...], l1.astype(BF16))
    a = jnp.where(strict, jnp.exp2(lb + tail), 0.0)
    acc_ref[...] = _dot(vt_ref[0, i], a.astype(BF16))
    carry0 = jnp.sum(l1, axis=0, keepdims=True)

    def body(j, carry):
        kb = i - 1 - j
        lb, l1 = log2_terms(_dot_nt(k_ref[0, kb], qz))
        tail = _dot(tri_ref[...], l1.astype(BF16)) + carry
        a = jnp.exp2(lb + tail)
        acc_ref[...] += _dot(vt_ref[0, kb], a.astype(BF16))
        return carry + jnp.sum(l1, axis=0, keepdims=True)

    lax.fori_loop(0, i, body, carry0)

    acc = acc_ref[...]
    o = jnp.concatenate([acc[:HEAD_DIM, :blk], acc[HEAD_DIM:, blk:]], axis=0).T
    o_ref[0] = (o * gate_ref[0].astype(F32)).astype(BF16)


def _out_kernel(x_ref, m_ref, w_ref, o_ref):
    o_ref[0] = x_ref[0] + _dot(m_ref[0], w_ref[...])


def _params(n_axes):
    return pltpu.CompilerParams(dimension_semantics=("arbitrary",) * n_axes,
                                vmem_limit_bytes=VMEM_LIMIT_BYTES)


def _const_spec(shape):
    nd = len(shape)
    return pl.BlockSpec(shape, lambda *_: (0,) * nd)


def kernel(x, norm_g, w_in_even, q_norm_g, k_norm_g, lam, subln_g, sgu_norm_g, w_s, b_s,
           w_out_even, w_in_odd, w_out_odd):
    bsz, seq, d = x.shape
    tm, blk = _tile_sizes(seq)
    nt, nb = seq // tm, seq // blk
    sds = jax.ShapeDtypeStruct

    reps = A_WIDTH // HEAD_DIM
    qg = (jnp.tile(q_norm_g[0], reps) * (HEAD_DIM ** -0.5 * LOG2E)).reshape(1, A_WIDTH)
    kg = jnp.tile(k_norm_g[0], reps).reshape(1, A_WIDTH)
    grp = jnp.arange(A_WIDTH) // HEAD_DIM
    gmat = jnp.where(grp[:, None] == grp[None, :], 1.0 / HEAD_DIM, 0.0).astype(BF16)
    pos = jnp.arange(B_CHUNK)
    ws = jnp.where((pos[:, None] // CHUNK) >= (pos[None, :] // CHUNK), w_s[0], 0.0).astype(BF16)
    bs_full = jnp.repeat(b_s[0].T, B_WIDTH // B_GROUPS, axis=1)
    slopes2 = jnp.exp2(-8.0 * jnp.arange(1, A_HEADS + 1, dtype=F32) / A_HEADS) * LOG2E
    kp = jnp.arange(blk)
    tri = (kp[None, :] > kp[:, None]).astype(BF16)

    row_spec = lambda w: pl.BlockSpec((1, tm, w), lambda b, t: (b, t, 0))
    vt_out_spec = lambda w: pl.BlockSpec((1, tm // blk, w, blk), lambda b, t: (b, t, 0, 0))

    q0, k0, vt0, ga0, mb0 = pl.pallas_call(
        functools.partial(_even_in_kernel, tm=tm, blk=blk),
        grid=(bsz, nt),
        in_specs=[row_spec(d), _const_spec((1, d)), _const_spec(w_in_even.shape[1:]),
                  _const_spec((1, A_WIDTH)), _const_spec((1, A_WIDTH)), _const_spec((A_WIDTH, A_WIDTH)),
                  _const_spec((1, B_WIDTH)), _const_spec(ws.shape), _const_spec(bs_full.shape)],
        out_specs=[row_spec(A_WIDTH), row_spec(A_WIDTH), vt_out_spec(A_WIDTH), row_spec(A_WIDTH),
                   row_spec(B_WIDTH)],
        out_shape=[sds((bsz, seq, A_WIDTH), BF16), sds((bsz, seq, A_WIDTH), BF16),
                   sds((bsz, nb, A_WIDTH, blk), BF16), sds((bsz, seq, A_WIDTH), BF16),
                   sds((bsz, seq, B_WIDTH), BF16)],
        compiler_params=_params(2), name="even_in",
    )(x, norm_g[0:1], w_in_even[0].astype(BF16), qg, kg, gmat, sgu_norm_g[0:1], ws, bs_full)

    qblk_spec = pl.BlockSpec((1, blk, LANES), lambda b, h, i: (b, i, h))
    kseq_spec = pl.BlockSpec((1, nb, blk, LANES), lambda b, h, i: (b, 0, 0, h))
    vtseq_spec = pl.BlockSpec((1, nb, LANES, blk), lambda b, h, i: (b, 0, h, 0))

    ma0 = pl.pallas_call(
        functools.partial(_diff_attn_kernel, blk=blk),
        grid=(bsz, A_HEADS, nb),
        in_specs=[pl.BlockSpec(memory_space=pltpu.SMEM), qblk_spec, kseq_spec, vtseq_spec, qblk_spec,
                  _const_spec(lam.shape[1:]), _const_spec((1, LANES))],
        out_specs=qblk_spec,
        out_shape=sds((bsz, seq, A_WIDTH), BF16),
        scratch_shapes=[pltpu.VMEM((LANES, 2 * blk), F32)],
        compiler_params=_params(3), name="diff_attn",
    )(slopes2, q0, k0.reshape(bsz, nb, blk, A_WIDTH), vt0, ga0, lam[0], subln_g[0:1])

    x1, q1, k1, vt1, gc1 = pl.pallas_call(
        functools.partial(_mid_kernel, tm=tm, blk=blk),
        grid=(bsz, nt),
        in_specs=[row_spec(d), row_spec(A_WIDTH), row_spec(B_WIDTH), _const_spec((d, d)),
                  _const_spec((1, d)), _const_spec(w_in_odd.shape[1:])],
        out_specs=[row_spec(d), row_spec(C_WIDTH), row_spec(C_WIDTH), vt_out_spec(C_WIDTH), row_spec(C_WIDTH)],
        out_shape=[sds((bsz, seq, d), F32), sds((bsz, seq, C_WIDTH), BF16), sds((bsz, seq, C_WIDTH), BF16),
                   sds((bsz, nb, C_WIDTH, blk), BF16), sds((bsz, seq, C_WIDTH), BF16)],
        compiler_params=_params(2), name="mid",
    )(x, ma0, mb0, w_out_even[0].astype(BF16), norm_g[1:2], w_in_odd[0].astype(BF16))

    mc1 = pl.pallas_call(
        functools.partial(_stick_kernel, blk=blk),
        grid=(bsz, C_WIDTH // LANES, nb),
        in_specs=[qblk_spec, kseq_spec, vtseq_spec, qblk_spec, _const_spec((blk, blk))],
        out_specs=qblk_spec,
        out_shape=sds((bsz, seq, C_WIDTH), BF16),
        scratch_shapes=[pltpu.VMEM((LANES, 2 * blk), F32)],
        compiler_params=_params(3), name="stick_attn",
    )(q1, k1.reshape(bsz, nb, blk, C_WIDTH), vt1, gc1, tri)

    return pl.pallas_call(
        _out_kernel,
        grid=(bsz, nt),
        in_specs=[row_spec(d), row_spec(C_WIDTH), _const_spec((C_WIDTH, d))],
        out_specs=row_spec(d),
        out_shape=sds((bsz, seq, d), F32),
        compiler_params=_params(2), name="out_proj",
    )(x1, mc1, w_out_odd[0].astype(BF16))
```

```python
import functools
import math

import jax
import jax.numpy as jnp
from jax import lax
from jax.experimental import pallas as pl
from jax.experimental.pallas import tpu as pltpu

EPS = 1e-6
LOG2E = 1.4426950408889634
CHUNK = 64
HEAD_DIM = 64
LANES = 128
BF16_ROWS = 16
A_HEADS = 4
A_WIDTH = 512
B_WIDTH = 512
B_CHUNK = 128
B_GROUPS = 4
C_WIDTH = 1024
LAMBDA_INIT_0 = 0.8 - 0.6 * math.exp(-0.3 * 0)
NEG_BIG = -1e30
VMEM_LIMIT_BYTES = 56 * 1024 * 1024
DIFF_GROUPS = 4
STICK_GROUPS = 4

F32 = jnp.float32
BF16 = jnp.bfloat16


def _tile_sizes(seq):
    blk = 256 if seq % 256 == 0 else 128
    tm = 512 if seq % 512 == 0 else blk
    qm = 2 if seq % (2 * blk) == 0 else 1
    return tm, blk, qm


def _dot(a, b):
    return jnp.dot(a, b, preferred_element_type=F32)


def _dot_nt(a, b):
    return lax.dot_general(a, b, (((1,), (1,)), ((), ())), preferred_element_type=F32)


def _silu(t):
    return t * (1.0 / (1.0 + jnp.exp(-t)))


def _rms(t, gain):
    ms = jnp.mean(t * t, axis=-1, keepdims=True)
    return t * lax.rsqrt(ms + EPS) * gain


def _split_halves(q):
    lane = lax.broadcasted_iota(jnp.int32, q.shape, 1)
    zero = jnp.zeros_like(q)
    return jnp.concatenate([jnp.where(lane < HEAD_DIM, q, zero), jnp.where(lane >= HEAD_DIM, q, zero)], axis=0)


def _even_in_kernel(x_ref, g_ref, w_ref, qg_ref, kg_ref, gmat_ref, sg_ref, ws_ref, bs_ref,
                    q_ref, k_ref, vt_ref, ga_ref, mb_ref, *, tm, blk):
    h = _rms(x_ref[0], g_ref[...]).astype(BF16)

    def proj(c0, width):
        return _dot(h, w_ref[:, c0:c0 + width])

    def head_norm(t, gain_ref):
        sq = t * t
        hi = sq.astype(BF16)
        lo = (sq - hi.astype(F32)).astype(BF16)
        ms = _dot(hi, gmat_ref[...]) + _dot(lo, gmat_ref[...])
        return t * lax.rsqrt(ms + EPS) * gain_ref[...]

    q_ref[0] = head_norm(proj(0, A_WIDTH), qg_ref).astype(BF16)
    k_ref[0] = head_norm(proj(A_WIDTH, A_WIDTH), kg_ref).astype(BF16)
    va = proj(2 * A_WIDTH, A_WIDTH)
    for j in range(tm // blk):
        vt_ref[0, j] = va[j * blk:(j + 1) * blk, :].T.astype(BF16)
    ga_ref[0] = _silu(proj(3 * A_WIDTH, A_WIDTH)).astype(BF16)

    base = 4 * A_WIDTH
    u = jax.nn.gelu(proj(base, B_WIDTH))
    vn = _rms(jax.nn.gelu(proj(base + B_WIDTH, B_WIDTH)), sg_ref[...]).astype(BF16)
    sgate = _silu(proj(base + 2 * B_WIDTH, B_WIDTH))
    gd = B_WIDTH // B_GROUPS
    for c in range(tm // B_CHUNK):
        rows = slice(c * B_CHUNK, (c + 1) * B_CHUNK)
        for g in range(B_GROUPS):
            cols = slice(g * gd, (g + 1) * gd)
            mixed = _dot(ws_ref[g], vn[rows, cols]) + bs_ref[:, cols]
            mb_ref[0, rows, cols] = (u[rows, cols] * mixed * sgate[rows, cols]).astype(BF16)


def _diff_attn_kernel(sl_ref, q_ref, k_ref, vt_ref, gate_ref, lam_ref, sub_ref, o_ref, acc_ref, kb_ref,
                      *, blk, qm, groups):
    hg = pl.program_id(1)
    i = pl.program_id(2)
    qblk = qm * blk
    n = 2 * qblk
    heads = range(groups)
    lanes = [slice(g * LANES, (g + 1) * LANES) for g in heads]
    slope2 = [sl_ref[hg * groups + g] for g in heads]
    qz = [_split_halves(q_ref[0, :, lanes[g]]) for g in heads]

    def update(ms, ls, e, shift, kb):
        m_new = [jnp.maximum(ms[g], jnp.max(e[g], axis=0, keepdims=True) + shift[g]) for g in heads]
        p = [jnp.exp2(e[g] - (m_new[g] - shift[g])) for g in heads]
        pv = [_dot(vt_ref[0, kb, lanes[g], :], p[g].astype(BF16)) for g in heads]
        alpha = [jnp.exp2(ms[g] - m_new[g]) for g in heads]
        for g in heads:
            acc_ref[g] = acc_ref[g] * alpha[g] + pv[g]
        l_new = [alpha[g] * ls[g] + jnp.sum(p[g], axis=0, keepdims=True) for g in heads]
        return tuple(m_new), tuple(l_new)

    krow = lax.broadcasted_iota(jnp.int32, (blk, LANES), 0).astype(F32)
    for g in heads:
        acc_ref[g] = jnp.zeros((LANES, n), F32)
        kb_ref[g] = slope2[g] * krow
    state = (tuple(jnp.full((1, n), NEG_BIG, F32) for _ in heads), tuple(jnp.zeros((1, n), F32) for _ in heads))

    kl = lax.broadcasted_iota(jnp.int32, (blk, n), 0)
    ql = lax.broadcasted_iota(jnp.int32, (blk, n), 1) & (qblk - 1)
    for d in range(qm):
        ka = kl + d * blk
        allowed = (ka // CHUNK) <= (ql // CHUNK)
        rel = (ql - jnp.abs(ka - ql)).astype(F32)
        kb = i * qm + d
        raw = [_dot_nt(k_ref[0, kb, :, lanes[g]], qz[g]) for g in heads]
        e = [jnp.where(allowed, raw[g] + slope2[g] * rel, NEG_BIG) for g in heads]
        state = update(*state, e, [0.0] * groups, kb)

    def body(kb, state):
        off = (kb * blk - i * qblk).astype(F32)
        raw = [_dot_nt(k_ref[0, kb, :, lanes[g]], qz[g]) for g in heads]
        e = [raw[g] + jnp.concatenate([kb_ref[g]] * (n // LANES), axis=1) for g in heads]
        return update(*state, e, [slope2[g] * off for g in heads], kb)

    _, ls = lax.fori_loop(0, i * qm, body, state)

    lam = lam_ref[...]
    lam_full = (jnp.exp(jnp.sum(lam[0:1] * lam[1:2], axis=1, keepdims=True))
                - jnp.exp(jnp.sum(lam[2:3] * lam[3:4], axis=1, keepdims=True)) + LAMBDA_INIT_0)
    for g in heads:
        acc = acc_ref[g] * (1.0 / ls[g])
        o = (acc[:, :qblk] - lam_full * acc[:, qblk:]).T
        o = _rms(o, sub_ref[...]) * (1.0 - LAMBDA_INIT_0)
        o_ref[0, :, lanes[g]] = (o * gate_ref[0, :, lanes[g]].astype(F32)).astype(BF16)


def _mid_kernel(x_ref, ma_ref, mb_ref, wo_ref, g_ref, wi_ref,
                x1_ref, q_ref, k_ref, vt_ref, gc_ref, *, tm, blk):
    x1 = x_ref[0] + _dot(ma_ref[0], wo_ref[0:A_WIDTH, :]) + _dot(mb_ref[0], wo_ref[A_WIDTH:, :])
    x1_ref[0] = x1
    h = _rms(x1, g_ref[...]).astype(BF16)

    def proj(c0):
        return _dot(h, wi_ref[:, c0:c0 + C_WIDTH])

    q_ref[0] = (proj(0) * (HEAD_DIM ** -0.5 * LOG2E)).astype(BF16)
    k_ref[0] = proj(C_WIDTH).astype(BF16)
    v = proj(2 * C_WIDTH)
    for j in range(tm // blk):
        vt_ref[0, j] = v[j * blk:(j + 1) * blk, :].T.astype(BF16)
    gc_ref[0] = _silu(proj(3 * C_WIDTH)).astype(BF16)


def _stick_kernel(q_ref, k_ref, vt_ref, gate_ref, tri_ref, o_ref, acc_ref, *, blk, qm, groups):
    i = pl.program_id(2)
    qblk = qm * blk
    n = 2 * qblk
    pairs = range(groups)
    lanes = [slice(g * LANES, (g + 1) * LANES) for g in pairs]
    qz = [_split_halves(q_ref[0, :, lanes[g]]) for g in pairs]
    sign = jnp.uint32(0x80000000)

    def softplus2(z):
        neg_abs = lax.bitcast_convert_type(lax.bitcast_convert_type(z, jnp.uint32) | sign, F32)
        return jnp.maximum(z, 0.0) + jnp.log(1.0 + jnp.exp2(neg_abs)) * LOG2E

    def step(carries, kb, valid):
        z = [_dot_nt(k_ref[0, kb, :, lanes[g]], qz[g]) for g in pairs]
        p = [softplus2(z[g]) for g in pairs]
        if valid is not None:
            p = [jnp.where(valid, p[g], 0.0) for g in pairs]
        s = [_dot(tri_ref[...], p[g].astype(BF16)) for g in pairs]
        a = [jnp.exp2(z[g] + s[g][:blk]) for g in pairs]
        if valid is not None:
            a = [jnp.where(valid, a[g], 0.0) for g in pairs]
        pv = [_dot(vt_ref[0, kb, lanes[g], :], a[g].astype(BF16)) for g in pairs]
        for g in pairs:
            acc_ref[g] += pv[g] * jnp.exp2(carries[g])
        return tuple(carries[g] + s[g][blk:blk + 1] for g in pairs)

    for g in pairs:
        acc_ref[g] = jnp.zeros((LANES, n), F32)
    carries = tuple(jnp.zeros((1, n), F32) for _ in pairs)

    kl = lax.broadcasted_iota(jnp.int32, (blk, n), 0)
    ql = lax.broadcasted_iota(jnp.int32, (blk, n), 1) & (qblk - 1)
    for d in reversed(range(qm)):
        carries = step(carries, i * qm + d, (kl + d * blk) < ql)

    lax.fori_loop(0, i * qm, lambda j, c: step(c, i * qm - 1 - j, None), carries)

    for g in pairs:
        acc = acc_ref[g]
        o = jnp.concatenate([acc[:HEAD_DIM, :qblk], acc[HEAD_DIM:, qblk:]], axis=0).T
        o_ref[0, :, lanes[g]] = (o * gate_ref[0, :, lanes[g]].astype(F32)).astype(BF16)


def _out_kernel(x_ref, m_ref, w_ref, o_ref):
    o_ref[0] = x_ref[0] + _dot(m_ref[0], w_ref[...])


def _params(n_axes):
    return pltpu.CompilerParams(dimension_semantics=("arbitrary",) * n_axes,
                                vmem_limit_bytes=VMEM_LIMIT_BYTES)


def _const_spec(shape):
    nd = len(shape)
    return pl.BlockSpec(shape, lambda *_: (0,) * nd)


def _attn_specs(nb, blk, qblk, groups):
    w = groups * LANES
    qspec = pl.BlockSpec((1, qblk, w), lambda b, h, i: (b, i, h))
    kseq = pl.BlockSpec((1, nb, blk, w), lambda b, h, i: (b, 0, 0, h))
    vtseq = pl.BlockSpec((1, nb, w, blk), lambda b, h, i: (b, 0, h, 0))
    return qspec, kseq, vtseq


def kernel(x, norm_g, w_in_even, q_norm_g, k_norm_g, lam, subln_g, sgu_norm_g, w_s, b_s,
           w_out_even, w_in_odd, w_out_odd):
    bsz, seq, d = x.shape
    tm, blk, qm = _tile_sizes(seq)
    qblk = qm * blk
    nt, nb, nq = seq // tm, seq // blk, seq // qblk
    sds = jax.ShapeDtypeStruct

    reps = A_WIDTH // HEAD_DIM
    qg = (jnp.tile(q_norm_g[0], reps) * (HEAD_DIM ** -0.5 * LOG2E)).reshape(1, A_WIDTH)
    kg = jnp.tile(k_norm_g[0], reps).reshape(1, A_WIDTH)
    grp = jnp.arange(A_WIDTH) // HEAD_DIM
    gmat = jnp.where(grp[:, None] == grp[None, :], 1.0 / HEAD_DIM, 0.0).astype(BF16)
    pos = jnp.arange(B_CHUNK)
    ws = jnp.where((pos[:, None] // CHUNK) >= (pos[None, :] // CHUNK), w_s[0], 0.0).astype(BF16)
    bs_full = jnp.repeat(b_s[0].T, B_WIDTH // B_GROUPS, axis=1)
    slopes2 = jnp.exp2(-8.0 * jnp.arange(1, A_HEADS + 1, dtype=F32) / A_HEADS) * LOG2E
    kp = jnp.arange(blk)
    tri = jnp.concatenate([jnp.where(kp[None, :] >= kp[:, None], -1.0, 0.0),
                           jnp.full((BF16_ROWS, blk), -1.0)], axis=0).astype(BF16)

    row_spec = lambda w: pl.BlockSpec((1, tm, w), lambda b, t: (b, t, 0))
    vt_out_spec = lambda w: pl.BlockSpec((1, tm // blk, w, blk), lambda b, t: (b, t, 0, 0))

    q0, k0, vt0, ga0, mb0 = pl.pallas_call(
        functools.partial(_even_in_kernel, tm=tm, blk=blk),
        grid=(bsz, nt),
        in_specs=[row_spec(d), _const_spec((1, d)), _const_spec(w_in_even.shape[1:]),
                  _const_spec((1, A_WIDTH)), _const_spec((1, A_WIDTH)), _const_spec((A_WIDTH, A_WIDTH)),
                  _const_spec((1, B_WIDTH)), _const_spec(ws.shape), _const_spec(bs_full.shape)],
        out_specs=[row_spec(A_WIDTH), row_spec(A_WIDTH), vt_out_spec(A_WIDTH), row_spec(A_WIDTH),
                   row_spec(B_WIDTH)],
        out_shape=[sds((bsz, seq, A_WIDTH), BF16), sds((bsz, seq, A_WIDTH), BF16),
                   sds((bsz, nb, A_WIDTH, blk), BF16), sds((bsz, seq, A_WIDTH), BF16),
                   sds((bsz, seq, B_WIDTH), BF16)],
        compiler_params=_params(2), name="even_in",
    )(x, norm_g[0:1], w_in_even[0].astype(BF16), qg, kg, gmat, sgu_norm_g[0:1], ws, bs_full)

    q_spec, kseq_spec, vtseq_spec = _attn_specs(nb, blk, qblk, DIFF_GROUPS)
    ma0 = pl.pallas_call(
        functools.partial(_diff_attn_kernel, blk=blk, qm=qm, groups=DIFF_GROUPS),
        grid=(bsz, A_HEADS // DIFF_GROUPS, nq),
        in_specs=[pl.BlockSpec(memory_space=pltpu.SMEM), q_spec, kseq_spec, vtseq_spec, q_spec,
                  _const_spec(lam.shape[1:]), _const_spec((1, LANES))],
        out_specs=q_spec,
        out_shape=sds((bsz, seq, A_WIDTH), BF16),
        scratch_shapes=[pltpu.VMEM((DIFF_GROUPS, LANES, 2 * qblk), F32),
                        pltpu.VMEM((DIFF_GROUPS, blk, LANES), F32)],
        compiler_params=_params(3), name="diff_attn",
    )(slopes2, q0, k0.reshape(bsz, nb, blk, A_WIDTH), vt0, ga0, lam[0], subln_g[0:1])

    x1, q1, k1, vt1, gc1 = pl.pallas_call(
        functools.partial(_mid_kernel, tm=tm, blk=blk),
        grid=(bsz, nt),
        in_specs=[row_spec(d), row_spec(A_WIDTH), row_spec(B_WIDTH), _const_spec((d, d)),
                  _const_spec((1, d)), _const_spec(w_in_odd.shape[1:])],
        out_specs=[row_spec(d), row_spec(C_WIDTH), row_spec(C_WIDTH), vt_out_spec(C_WIDTH), row_spec(C_WIDTH)],
        out_shape=[sds((bsz, seq, d), F32), sds((bsz, seq, C_WIDTH), BF16), sds((bsz, seq, C_WIDTH), BF16),
                   sds((bsz, nb, C_WIDTH, blk), BF16), sds((bsz, seq, C_WIDTH), BF16)],
        compiler_params=_params(2), name="mid",
    )(x, ma0, mb0, w_out_even[0].astype(BF16), norm_g[1:2], w_in_odd[0].astype(BF16))

    q_spec, kseq_spec, vtseq_spec = _attn_specs(nb, blk, qblk, STICK_GROUPS)
    mc1 = pl.pallas_call(
        functools.partial(_stick_kernel, blk=blk, qm=qm, groups=STICK_GROUPS),
        grid=(bsz, C_WIDTH // LANES // STICK_GROUPS, nq),
        in_specs=[q_spec, kseq_spec, vtseq_spec, q_spec, _const_spec(tri.shape)],
        out_specs=q_spec,
        out_shape=sds((bsz, seq, C_WIDTH), BF16),
        scratch_shapes=[pltpu.VMEM((STICK_GROUPS, LANES, 2 * qblk), F32)],
        compiler_params=_params(3), name="stick_attn",
    )(q1, k1.reshape(bsz, nb, blk, C_WIDTH), vt1, gc1, tri)

    return pl.pallas_call(
        _out_kernel,
        grid=(bsz, nt),
        in_specs=[row_spec(d), row_spec(C_WIDTH), _const_spec((C_WIDTH, d))],
        out_specs=row_spec(d),
        out_shape=sds((bsz, seq, d), F32),
        compiler_params=_params(2), name="out_proj",
    )(x1, mc1, w_out_odd[0].astype(BF16))
```

```python
import functools
import math

import jax
import jax.numpy as jnp
from jax import lax
from jax.experimental import pallas as pl
from jax.experimental.pallas import tpu as pltpu

EPS = 1e-6
LOG2E = 1.4426950408889634
CHUNK = 64
HEAD_DIM = 64
LANES = 128
MXU_COLS = 256
BF16_ROWS = 16
A_HEADS = 4
A_WIDTH = 512
B_WIDTH = 512
B_CHUNK = 128
B_GROUPS = 4
C_WIDTH = 1024
LAMBDA_INIT_0 = 0.8 - 0.6 * math.exp(-0.3 * 0)
NEG_BIG = -1e30
EXP2_UNDERFLOW = -160.0
NORM_MARGIN = 1.02
VMEM_LIMIT_BYTES = 56 * 1024 * 1024
DIFF_GROUPS = 4
STICK_GROUPS = 4
LOOKAHEAD = 2
SKIP = "skip"

F32 = jnp.float32
BF16 = jnp.bfloat16


def _tile_sizes(seq):
    blk = 256 if seq % 256 == 0 else 128
    tm = 512 if seq % 512 == 0 else blk
    qm = 2 if seq % (2 * blk) == 0 else 1
    return tm, blk, qm


def _dot(a, b):
    return jnp.dot(a, b, preferred_element_type=F32)


def _dot_nt(a, b):
    return lax.dot_general(a, b, (((1,), (1,)), ((), ())), preferred_element_type=F32)


def _silu(t):
    return t * (1.0 / (1.0 + jnp.exp(-t)))


def _rms(t, gain):
    ms = jnp.mean(t * t, axis=-1, keepdims=True)
    return t * lax.rsqrt(ms + EPS) * gain


def _split_halves(q):
    lane = lax.broadcasted_iota(jnp.int32, q.shape, 1)
    zero = jnp.zeros_like(q)
    return jnp.concatenate([jnp.where(lane < HEAD_DIM, q, zero), jnp.where(lane >= HEAD_DIM, q, zero)], axis=0)


def _tile(ct):
    return slice(ct * MXU_COLS, (ct + 1) * MXU_COLS)


def _tile_positions(blk, qblk, d, ct):
    ka = lax.broadcasted_iota(jnp.int32, (blk, MXU_COLS), 0) + d * blk
    ql = (lax.broadcasted_iota(jnp.int32, (blk, MXU_COLS), 1) + ct * MXU_COLS) & (qblk - 1)
    q_lo = (ct * MXU_COLS) % qblk
    return ka, ql, q_lo, q_lo + MXU_COLS - 1


def _pipeline(items, kinds, qk, process):
    active = [k for k in range(len(items)) if not isinstance(kinds[items[k][1]], str)]
    raw = {k: qk(k) for k in active[:LOOKAHEAD]}
    for idx, k in enumerate(active):
        if idx + LOOKAHEAD < len(active):
            nxt = active[idx + LOOKAHEAD]
            raw[nxt] = qk(nxt)
        process(k, raw.pop(k))


def _even_in_kernel(x_ref, g_ref, w_ref, qg_ref, kg_ref, gmat_ref, sg_ref, ws_ref, bs_ref,
                    q_ref, k_ref, vt_ref, ga_ref, mb_ref, *, tm, blk):
    h = _rms(x_ref[0], g_ref[...]).astype(BF16)

    def proj(c0, width):
        return _dot(h, w_ref[:, c0:c0 + width])

    def head_norm(t, gain_ref):
        sq = t * t
        hi = sq.astype(BF16)
        lo = (sq - hi.astype(F32)).astype(BF16)
        ms = _dot(hi, gmat_ref[...]) + _dot(lo, gmat_ref[...])
        return t * lax.rsqrt(ms + EPS) * gain_ref[...]

    q_ref[0] = head_norm(proj(0, A_WIDTH), qg_ref).astype(BF16)
    k_ref[0] = head_norm(proj(A_WIDTH, A_WIDTH), kg_ref).astype(BF16)
    va = proj(2 * A_WIDTH, A_WIDTH)
    for j in range(tm // blk):
        vt_ref[0, j] = va[j * blk:(j + 1) * blk, :].T.astype(BF16)
    ga_ref[0] = _silu(proj(3 * A_WIDTH, A_WIDTH)).astype(BF16)

    base = 4 * A_WIDTH
    u = jax.nn.gelu(proj(base, B_WIDTH))
    vn = _rms(jax.nn.gelu(proj(base + B_WIDTH, B_WIDTH)), sg_ref[...]).astype(BF16)
    sgate = _silu(proj(base + 2 * B_WIDTH, B_WIDTH))
    gd = B_WIDTH // B_GROUPS
    for c in range(tm // B_CHUNK):
        rows = slice(c * B_CHUNK, (c + 1) * B_CHUNK)
        for g in range(B_GROUPS):
            cols = slice(g * gd, (g + 1) * gd)
            mixed = _dot(ws_ref[g], vn[rows, cols]) + bs_ref[:, cols]
            mb_ref[0, rows, cols] = (u[rows, cols] * mixed * sgate[rows, cols]).astype(BF16)


def _diff_attn_kernel(sl_ref, rmax_ref, q_ref, k_ref, vt_ref, gate_ref, lam_ref, sub_ref, o_ref, acc_ref, kb_ref,
                      *, blk, qm, groups):
    hg = pl.program_id(1)
    i = pl.program_id(2)
    qblk = qm * blk
    n = 2 * qblk
    heads = range(groups)
    lanes = [slice(g * LANES, (g + 1) * LANES) for g in heads]
    slope2 = [sl_ref[hg * groups + g] for g in heads]
    qz = [_split_halves(q_ref[0, :, lanes[g]]) for g in heads]

    def update(hs, ms, ls, e, shift, kb):
        m_new = {g: jnp.maximum(ms[g], jnp.max(e[g], axis=0, keepdims=True) + shift[g]) for g in hs}
        p = {g: jnp.exp2(e[g] - (m_new[g] - shift[g])) for g in hs}
        pv = {g: _dot(vt_ref[0, kb, lanes[g], :], p[g].astype(BF16)) for g in hs}
        alpha = {g: jnp.exp2(ms[g] - m_new[g]) for g in hs}
        for g in hs:
            acc_ref[g] = acc_ref[g] * alpha[g] + pv[g]
        l_new = {g: alpha[g] * ls[g] + jnp.sum(p[g], axis=0, keepdims=True) for g in hs}
        return tuple(m_new.get(g, ms[g]) for g in heads), tuple(l_new.get(g, ls[g]) for g in heads)

    krow = lax.broadcasted_iota(jnp.int32, (blk, LANES), 0).astype(F32)
    for g in heads:
        acc_ref[g] = jnp.zeros((LANES, n), F32)
        kb_ref[g] = slope2[g] * krow
    state = (tuple(jnp.full((1, n), NEG_BIG, F32) for _ in heads), tuple(jnp.zeros((1, n), F32) for _ in heads))

    kl = lax.broadcasted_iota(jnp.int32, (blk, n), 0)
    ql = lax.broadcasted_iota(jnp.int32, (blk, n), 1) & (qblk - 1)
    for d in range(qm):
        ka = kl + d * blk
        allowed = (ka // CHUNK) <= (ql // CHUNK)
        rel = (ql - jnp.abs(ka - ql)).astype(F32)
        kb = i * qm + d
        raw = [_dot_nt(k_ref[0, kb, :, lanes[g]], qz[g]) for g in heads]
        e = [jnp.where(allowed, raw[g] + slope2[g] * rel, NEG_BIG) for g in heads]
        state = update(heads, *state, e, [0.0] * groups, kb)

    total = i * qm
    rmax = rmax_ref[0]

    def walk(hs, watched, state):
        def reach(j, mmin):
            far = (j * blk).astype(F32)
            alive = [rmax - slope2[g] * far - mmin[g] > EXP2_UNDERFLOW for g in watched]
            return jnp.logical_and(j < total, functools.reduce(jnp.logical_or, alive))

        def body(carry):
            j, ms, ls, _ = carry
            kb = total - 1 - j
            off = (kb * blk - i * qblk).astype(F32)
            raw = {g: _dot_nt(k_ref[0, kb, :, lanes[g]], qz[g]) for g in hs}
            e = {g: raw[g] + jnp.concatenate([kb_ref[g]] * (n // LANES), axis=1) for g in hs}
            mmin = tuple(jnp.min(ms[g]) for g in heads)
            ms, ls = update(hs, ms, ls, e, {g: slope2[g] * off for g in hs}, kb)
            return j + 1, ms, ls, mmin

        j, ms, ls = state
        mmin = tuple(jnp.min(ms[g]) for g in heads)
        j, ms, ls, _ = lax.while_loop(lambda c: reach(c[0], c[3]), body, (j, ms, ls, mmin))
        return j, ms, ls

    first, second = list(heads)[:groups // 2], list(heads)[groups // 2:]
    state = walk(first + second, first, (jnp.int32(0),) + state)
    _, _, ls = walk(second, second, state)

    lam = lam_ref[...]
    lam_full = (jnp.exp(jnp.sum(lam[0:1] * lam[1:2], axis=1, keepdims=True))
                - jnp.exp(jnp.sum(lam[2:3] * lam[3:4], axis=1, keepdims=True)) + LAMBDA_INIT_0)
    for g in heads:
        acc = acc_ref[g] * (1.0 / ls[g])
        o = (acc[:, :qblk] - lam_full * acc[:, qblk:]).T
        o = _rms(o, sub_ref[...]) * (1.0 - LAMBDA_INIT_0)
        o_ref[0, :, lanes[g]] = (o * gate_ref[0, :, lanes[g]].astype(F32)).astype(BF16)


def _mid_kernel(x_ref, ma_ref, mb_ref, wo_ref, g_ref, wi_ref,
                x1_ref, q_ref, k_ref, vt_ref, gc_ref, *, tm, blk):
    x1 = x_ref[0] + _dot(ma_ref[0], wo_ref[0:A_WIDTH, :]) + _dot(mb_ref[0], wo_ref[A_WIDTH:, :])
    x1_ref[0] = x1
    h = _rms(x1, g_ref[...]).astype(BF16)

    def proj(c0):
        return _dot(h, wi_ref[:, c0:c0 + C_WIDTH])

    q_ref[0] = (proj(0) * (HEAD_DIM ** -0.5 * LOG2E)).astype(BF16)
    k_ref[0] = proj(C_WIDTH).astype(BF16)
    v = proj(2 * C_WIDTH)
    for j in range(tm // blk):
        vt_ref[0, j] = v[j * blk:(j + 1) * blk, :].T.astype(BF16)
    gc_ref[0] = _silu(proj(3 * C_WIDTH)).astype(BF16)


def _stick_kernel(q_ref, k_ref, vt_ref, gate_ref, tri_ref, o_ref, acc_ref, c_ref, qz_ref, *, blk, qm, groups):
    i = pl.program_id(2)
    qblk = qm * blk
    n = 2 * qblk
    pairs = range(groups)
    lanes = [slice(g * LANES, (g + 1) * LANES) for g in pairs]
    items = [(g, ct) for g in pairs for ct in range(n // MXU_COLS)]

    for g in pairs:
        qz_ref[g] = _split_halves(q_ref[0, :, lanes[g]])
        acc_ref[g] = jnp.zeros((LANES, n), F32)
        c_ref[g] = jnp.zeros((1, n), F32)

    def softplus2(z):
        return jnp.maximum(z, 0.0) + jnp.log(1.0 + jnp.exp2(-jnp.abs(z))) * LOG2E

    def step(kb, kinds):
        def qk(k):
            g, ct = items[k]
            return _dot_nt(k_ref[0, kb, :, lanes[g]], qz_ref[g, _tile(ct), :])

        pending = []

        def finish():
            k, z, s = pending.pop()
            g, ct = items[k]
            a = jnp.exp2(z + s[:blk])
            if kinds[ct] is not None:
                a = jnp.where(kinds[ct], a, 0.0)
            pv = _dot(vt_ref[0, kb, lanes[g], :], a.astype(BF16))
            carry = c_ref[g, :, _tile(ct)]
            acc_ref[g, :, _tile(ct)] += pv * jnp.exp2(carry)
            c_ref[g, :, _tile(ct)] = carry + s[blk:blk + 1]

        def process(k, z):
            ct = items[k][1]
            p = softplus2(z)
            if kinds[ct] is not None:
                p = jnp.where(kinds[ct], p, 0.0)
            s = _dot(tri_ref[...], p.astype(BF16))
            if pending:
                finish()
            pending.append((k, z, s))

        _pipeline(items, kinds, qk, process)
        finish()

    for d in reversed(range(qm)):
        kinds = {}
        for ct in range(n // MXU_COLS):
            ka, ql, q_lo, q_hi = _tile_positions(blk, qblk, d, ct)
            if d * blk >= q_hi:
                kinds[ct] = SKIP
            elif d * blk + blk - 1 < q_lo:
                kinds[ct] = None
            else:
                kinds[ct] = ka < ql
        step(i * qm + d, kinds)

    all_open = {ct: None for ct in range(n // MXU_COLS)}

    def live(j):
        return jnp.logical_and(j < i * qm, jnp.max(c_ref[...]) > EXP2_UNDERFLOW)

    def body(j):
        step(i * qm - 1 - j, all_open)
        return j + 1

    lax.while_loop(live, body, jnp.int32(0))

    for g in pairs:
        acc = acc_ref[g]
        o = jnp.concatenate([acc[:HEAD_DIM, :qblk], acc[HEAD_DIM:, qblk:]], axis=0).T
        o_ref[0, :, lanes[g]] = (o * gate_ref[0, :, lanes[g]].astype(F32)).astype(BF16)


def _out_kernel(x_ref, m_ref, w_ref, o_ref):
    o_ref[0] = x_ref[0] + _dot(m_ref[0], w_ref[...])


def _params(n_axes):
    return pltpu.CompilerParams(dimension_semantics=("arbitrary",) * n_axes,
                                vmem_limit_bytes=VMEM_LIMIT_BYTES)


def _const_spec(shape):
    nd = len(shape)
    return pl.BlockSpec(shape, lambda *_: (0,) * nd)


def _attn_specs(nb, blk, qblk, groups):
    w = groups * LANES
    qspec = pl.BlockSpec((1, qblk, w), lambda b, h, i: (b, i, h))
    kseq = pl.BlockSpec((1, nb, blk, w), lambda b, h, i: (b, 0, 0, h))
    vtseq = pl.BlockSpec((1, nb, w, blk), lambda b, h, i: (b, 0, h, 0))
    return qspec, kseq, vtseq


def kernel(x, norm_g, w_in_even, q_norm_g, k_norm_g, lam, subln_g, sgu_norm_g, w_s, b_s,
           w_out_even, w_in_odd, w_out_odd):
    bsz, seq, d = x.shape
    tm, blk, qm = _tile_sizes(seq)
    qblk = qm * blk
    nt, nb, nq = seq // tm, seq // blk, seq // qblk
    sds = jax.ShapeDtypeStruct

    reps = A_WIDTH // HEAD_DIM
    qg = (jnp.tile(q_norm_g[0], reps) * (HEAD_DIM ** -0.5 * LOG2E)).reshape(1, A_WIDTH)
    kg = jnp.tile(k_norm_g[0], reps).reshape(1, A_WIDTH)
    grp = jnp.arange(A_WIDTH) // HEAD_DIM
    gmat = jnp.where(grp[:, None] == grp[None, :], 1.0 / HEAD_DIM, 0.0).astype(BF16)
    pos = jnp.arange(B_CHUNK)
    ws = jnp.where((pos[:, None] // CHUNK) >= (pos[None, :] // CHUNK), w_s[0], 0.0).astype(BF16)
    bs_full = jnp.repeat(b_s[0].T, B_WIDTH // B_GROUPS, axis=1)
    slopes2 = jnp.exp2(-8.0 * jnp.arange(1, A_HEADS + 1, dtype=F32) / A_HEADS) * LOG2E
    rmax = (NORM_MARGIN * HEAD_DIM * jnp.max(jnp.abs(qg)) * jnp.max(jnp.abs(kg))).reshape(1)
    kp = jnp.arange(blk)
    tri = jnp.concatenate([jnp.where(kp[None, :] >= kp[:, None], -1.0, 0.0),
                           jnp.full((BF16_ROWS, blk), -1.0)], axis=0).astype(BF16)

    row_spec = lambda w: pl.BlockSpec((1, tm, w), lambda b, t: (b, t, 0))
    vt_out_spec = lambda w: pl.BlockSpec((1, tm // blk, w, blk), lambda b, t: (b, t, 0, 0))

    q0, k0, vt0, ga0, mb0 = pl.pallas_call(
        functools.partial(_even_in_kernel, tm=tm, blk=blk),
        grid=(bsz, nt),
        in_specs=[row_spec(d), _const_spec((1, d)), _const_spec(w_in_even.shape[1:]),
                  _const_spec((1, A_WIDTH)), _const_spec((1, A_WIDTH)), _const_spec((A_WIDTH, A_WIDTH)),
                  _const_spec((1, B_WIDTH)), _const_spec(ws.shape), _const_spec(bs_full.shape)],
        out_specs=[row_spec(A_WIDTH), row_spec(A_WIDTH), vt_out_spec(A_WIDTH), row_spec(A_WIDTH),
                   row_spec(B_WIDTH)],
        out_shape=[sds((bsz, seq, A_WIDTH), BF16), sds((bsz, seq, A_WIDTH), BF16),
                   sds((bsz, nb, A_WIDTH, blk), BF16), sds((bsz, seq, A_WIDTH), BF16),
                   sds((bsz, seq, B_WIDTH), BF16)],
        compiler_params=_params(2), name="even_in",
    )(x, norm_g[0:1], w_in_even[0].astype(BF16), qg, kg, gmat, sgu_norm_g[0:1], ws, bs_full)

    q_spec, kseq_spec, vtseq_spec = _attn_specs(nb, blk, qblk, DIFF_GROUPS)
    ma0 = pl.pallas_call(
        functools.partial(_diff_attn_kernel, blk=blk, qm=qm, groups=DIFF_GROUPS),
        grid=(bsz, A_HEADS // DIFF_GROUPS, nq),
        in_specs=[pl.BlockSpec(memory_space=pltpu.SMEM), pl.BlockSpec(memory_space=pltpu.SMEM),
                  q_spec, kseq_spec, vtseq_spec, q_spec, _const_spec(lam.shape[1:]), _const_spec((1, LANES))],
        out_specs=q_spec,
        out_shape=sds((bsz, seq, A_WIDTH), BF16),
        scratch_shapes=[pltpu.VMEM((DIFF_GROUPS, LANES, 2 * qblk), F32),
                        pltpu.VMEM((DIFF_GROUPS, blk, LANES), F32)],
        compiler_params=_params(3), name="diff_attn",
    )(slopes2, rmax, q0, k0.reshape(bsz, nb, blk, A_WIDTH), vt0, ga0, lam[0], subln_g[0:1])

    x1, q1, k1, vt1, gc1 = pl.pallas_call(
        functools.partial(_mid_kernel, tm=tm, blk=blk),
        grid=(bsz, nt),
        in_specs=[row_spec(d), row_spec(A_WIDTH), row_spec(B_WIDTH), _const_spec((d, d)),
                  _const_spec((1, d)), _const_spec(w_in_odd.shape[1:])],
        out_specs=[row_spec(d), row_spec(C_WIDTH), row_spec(C_WIDTH), vt_out_spec(C_WIDTH), row_spec(C_WIDTH)],
        out_shape=[sds((bsz, seq, d), F32), sds((bsz, seq, C_WIDTH), BF16), sds((bsz, seq, C_WIDTH), BF16),
                   sds((bsz, nb, C_WIDTH, blk), BF16), sds((bsz, seq, C_WIDTH), BF16)],
        compiler_params=_params(2), name="mid",
    )(x, ma0, mb0, w_out_even[0].astype(BF16), norm_g[1:2], w_in_odd[0].astype(BF16))

    q_spec, kseq_spec, vtseq_spec = _attn_specs(nb, blk, qblk, STICK_GROUPS)
    mc1 = pl.pallas_call(
        functools.partial(_stick_kernel, blk=blk, qm=qm, groups=STICK_GROUPS),
        grid=(bsz, C_WIDTH // LANES // STICK_GROUPS, nq),
        in_specs=[q_spec, kseq_spec, vtseq_spec, q_spec, _const_spec(tri.shape)],
        out_specs=q_spec,
        out_shape=sds((bsz, seq, C_WIDTH), BF16),
        scratch_shapes=[pltpu.VMEM((STICK_GROUPS, LANES, 2 * qblk), F32),
                        pltpu.VMEM((STICK_GROUPS, 1, 2 * qblk), F32),
                        pltpu.VMEM((STICK_GROUPS, 2 * qblk, LANES), BF16)],
        compiler_params=_params(3), name="stick_attn",
    )(q1, k1.reshape(bsz, nb, blk, C_WIDTH), vt1, gc1, tri)

    return pl.pallas_call(
        _out_kernel,
        grid=(bsz, nt),
        in_specs=[row_spec(d), row_spec(C_WIDTH), _const_spec((C_WIDTH, d))],
        out_specs=row_spec(d),
        out_shape=sds((bsz, seq, d), F32),
        compiler_params=_params(2), name="out_proj",
    )(x1, mc1, w_out_odd[0].astype(BF16))
```

```python
import functools
import math

import jax
import jax.numpy as jnp
from jax import lax
from jax.experimental import pallas as pl
from jax.experimental.pallas import tpu as pltpu

EPS = 1e-6
LOG2E = 1.4426950408889634
CHUNK = 64
HEAD_DIM = 64
LANES = 128
MXU_COLS = 256
BF16_ROWS = 16
A_HEADS = 4
A_WIDTH = 512
B_WIDTH = 512
B_CHUNK = 128
B_GROUPS = 4
C_WIDTH = 1024
LAMBDA_INIT_0 = 0.8 - 0.6 * math.exp(-0.3 * 0)
NEG_BIG = -1e30
EXP2_UNDERFLOW = -160.0
FROZEN_MAX_LIMIT = 40.0
NORM_MARGIN = 1.02
VMEM_LIMIT_BYTES = 56 * 1024 * 1024
DIFF_GROUPS = 4
STICK_GROUPS = 4
LOOKAHEAD = 2
SKIP = "skip"

F32 = jnp.float32
BF16 = jnp.bfloat16


def _tile_sizes(seq):
    blk = 256 if seq % 256 == 0 else 128
    tm = 512 if seq % 512 == 0 else blk
    qm = 2 if seq % (2 * blk) == 0 else 1
    return tm, blk, qm


def _dot(a, b):
    return jnp.dot(a, b, preferred_element_type=F32)


def _dot_nt(a, b):
    return lax.dot_general(a, b, (((1,), (1,)), ((), ())), preferred_element_type=F32)


def _silu(t):
    return t * (1.0 / (1.0 + jnp.exp(-t)))


def _rms(t, gain):
    ms = jnp.mean(t * t, axis=-1, keepdims=True)
    return t * lax.rsqrt(ms + EPS) * gain


def _split_halves(q):
    lane = lax.broadcasted_iota(jnp.int32, q.shape, 1)
    zero = jnp.zeros_like(q)
    return jnp.concatenate([jnp.where(lane < HEAD_DIM, q, zero), jnp.where(lane >= HEAD_DIM, q, zero)], axis=0)


def _tile(ct):
    return slice(ct * MXU_COLS, (ct + 1) * MXU_COLS)


def _tile_positions(blk, qblk, d, ct):
    ka = lax.broadcasted_iota(jnp.int32, (blk, MXU_COLS), 0) + d * blk
    ql = (lax.broadcasted_iota(jnp.int32, (blk, MXU_COLS), 1) + ct * MXU_COLS) & (qblk - 1)
    q_lo = (ct * MXU_COLS) % qblk
    return ka, ql, q_lo, q_lo + MXU_COLS - 1


def _pipeline(items, kinds, qk, process):
    active = [k for k in range(len(items)) if not isinstance(kinds[items[k][1]], str)]
    raw = {k: qk(k) for k in active[:LOOKAHEAD]}
    for idx, k in enumerate(active):
        if idx + LOOKAHEAD < len(active):
            nxt = active[idx + LOOKAHEAD]
            raw[nxt] = qk(nxt)
        process(k, raw.pop(k))


def _even_in_kernel(x_ref, g_ref, w_ref, qg_ref, kg_ref, gmat_ref, sg_ref, ws_ref, bs_ref,
                    q_ref, k_ref, vt_ref, ga_ref, mb_ref, *, tm, blk):
    h = _rms(x_ref[0], g_ref[...]).astype(BF16)

    def proj(c0, width):
        return _dot(h, w_ref[:, c0:c0 + width])

    def head_norm(t, gain_ref):
        sq = t * t
        hi = sq.astype(BF16)
        lo = (sq - hi.astype(F32)).astype(BF16)
        ms = _dot(hi, gmat_ref[...]) + _dot(lo, gmat_ref[...])
        return t * lax.rsqrt(ms + EPS) * gain_ref[...]

    q_ref[0] = head_norm(proj(0, A_WIDTH), qg_ref).astype(BF16)
    k_ref[0] = head_norm(proj(A_WIDTH, A_WIDTH), kg_ref).astype(BF16)
    va = proj(2 * A_WIDTH, A_WIDTH)
    for j in range(tm // blk):
        vt_ref[0, j] = va[j * blk:(j + 1) * blk, :].T.astype(BF16)
    ga_ref[0] = _silu(proj(3 * A_WIDTH, A_WIDTH)).astype(BF16)

    base = 4 * A_WIDTH
    u = jax.nn.gelu(proj(base, B_WIDTH))
    vn = _rms(jax.nn.gelu(proj(base + B_WIDTH, B_WIDTH)), sg_ref[...]).astype(BF16)
    sgate = _silu(proj(base + 2 * B_WIDTH, B_WIDTH))
    gd = B_WIDTH // B_GROUPS
    for c in range(tm // B_CHUNK):
        rows = slice(c * B_CHUNK, (c + 1) * B_CHUNK)
        for g in range(B_GROUPS):
            cols = slice(g * gd, (g + 1) * gd)
            mixed = _dot(ws_ref[g], vn[rows, cols]) + bs_ref[:, cols]
            mb_ref[0, rows, cols] = (u[rows, cols] * mixed * sgate[rows, cols]).astype(BF16)


def _diff_attn_kernel(sl_ref, rmax_ref, q_ref, k_ref, vt_ref, gate_ref, lam_ref, sub_ref, o_ref, acc_ref, kb_ref,
                      *, blk, qm, groups):
    hg = pl.program_id(1)
    i = pl.program_id(2)
    qblk = qm * blk
    n = 2 * qblk
    heads = range(groups)
    lanes = [slice(g * LANES, (g + 1) * LANES) for g in heads]
    slope2 = [sl_ref[hg * groups + g] for g in heads]
    qz = [_split_halves(q_ref[0, :, lanes[g]]) for g in heads]

    def update(hs, ms, ls, e, shift, kb, frozen=False):
        if frozen:
            p = {g: jnp.exp2(e[g] - (ms[g] - shift[g])) for g in hs}
            pv = {g: _dot(vt_ref[0, kb, lanes[g], :], p[g].astype(BF16)) for g in hs}
            for g in hs:
                acc_ref[g] += pv[g]
            l_new = {g: ls[g] + jnp.sum(p[g], axis=0, keepdims=True) for g in hs}
            return ms, tuple(l_new.get(g, ls[g]) for g in heads)
        m_new = {g: jnp.maximum(ms[g], jnp.max(e[g], axis=0, keepdims=True) + shift[g]) for g in hs}
        p = {g: jnp.exp2(e[g] - (m_new[g] - shift[g])) for g in hs}
        pv = {g: _dot(vt_ref[0, kb, lanes[g], :], p[g].astype(BF16)) for g in hs}
        alpha = {g: jnp.exp2(ms[g] - m_new[g]) for g in hs}
        for g in hs:
            acc_ref[g] = acc_ref[g] * alpha[g] + pv[g]
        l_new = {g: alpha[g] * ls[g] + jnp.sum(p[g], axis=0, keepdims=True) for g in hs}
        return tuple(m_new.get(g, ms[g]) for g in heads), tuple(l_new.get(g, ls[g]) for g in heads)

    krow = lax.broadcasted_iota(jnp.int32, (blk, LANES), 0).astype(F32)
    for g in heads:
        acc_ref[g] = jnp.zeros((LANES, n), F32)
        kb_ref[g] = slope2[g] * krow
    state = (tuple(jnp.full((1, n), NEG_BIG, F32) for _ in heads), tuple(jnp.zeros((1, n), F32) for _ in heads))

    kl = lax.broadcasted_iota(jnp.int32, (blk, n), 0)
    ql = lax.broadcasted_iota(jnp.int32, (blk, n), 1) & (qblk - 1)
    for d in range(qm):
        ka = kl + d * blk
        allowed = (ka // CHUNK) <= (ql // CHUNK)
        rel = (ql - jnp.abs(ka - ql)).astype(F32)
        kb = i * qm + d
        raw = [_dot_nt(k_ref[0, kb, :, lanes[g]], qz[g]) for g in heads]
        e = [jnp.where(allowed, raw[g] + slope2[g] * rel, NEG_BIG) for g in heads]
        state = update(heads, *state, e, [0.0] * groups, kb)

    total = i * qm
    rmax = rmax_ref[0]

    def walk(hs, watched, state, frozen):
        def reach(j, mmin):
            far = (j * blk).astype(F32)
            alive = [rmax - slope2[g] * far - mmin[g] > EXP2_UNDERFLOW for g in watched]
            return jnp.logical_and(j < total, functools.reduce(jnp.logical_or, alive))

        def body(carry):
            j, ms, ls, mmin = carry
            kb = total - 1 - j
            off = (kb * blk - i * qblk).astype(F32)
            raw = {g: _dot_nt(k_ref[0, kb, :, lanes[g]], qz[g]) for g in hs}
            e = {g: raw[g] + jnp.concatenate([kb_ref[g]] * (n // LANES), axis=1) for g in hs}
            if not frozen:
                mmin = tuple(jnp.min(ms[g]) for g in heads)
            ms, ls = update(hs, ms, ls, e, {g: slope2[g] * off for g in hs}, kb, frozen)
            return j + 1, ms, ls, mmin

        j, ms, ls = state
        mmin = tuple(jnp.min(ms[g]) for g in heads)
        j, ms, ls, _ = lax.while_loop(lambda c: reach(c[0], c[3]), body, (j, ms, ls, mmin))
        return j, ms, ls

    hlist = list(heads)
    phases = [(hlist[k:], [k]) for k in range(groups - 2)] + [(hlist[groups - 2:], hlist[groups - 2:])]

    def walks(frozen):
        def run():
            st = (jnp.int32(0),) + state
            for hs, watched in phases:
                st = walk(hs, watched, st, frozen)
            return st[2]
        return run

    ls = lax.cond(rmax <= FROZEN_MAX_LIMIT, walks(True), walks(False))

    lam = lam_ref[...]
    lam_full = (jnp.exp(jnp.sum(lam[0:1] * lam[1:2], axis=1, keepdims=True))
                - jnp.exp(jnp.sum(lam[2:3] * lam[3:4], axis=1, keepdims=True)) + LAMBDA_INIT_0)
    for g in heads:
        acc = acc_ref[g] * (1.0 / ls[g])
        o = (acc[:, :qblk] - lam_full * acc[:, qblk:]).T
        o = _rms(o, sub_ref[...]) * (1.0 - LAMBDA_INIT_0)
        o_ref[0, :, lanes[g]] = (o * gate_ref[0, :, lanes[g]].astype(F32)).astype(BF16)


def _mid_kernel(x_ref, ma_ref, mb_ref, wo_ref, g_ref, wi_ref,
                x1_ref, q_ref, k_ref, vt_ref, gc_ref, *, tm, blk):
    x1 = x_ref[0] + _dot(ma_ref[0], wo_ref[0:A_WIDTH, :]) + _dot(mb_ref[0], wo_ref[A_WIDTH:, :])
    x1_ref[0] = x1
    h = _rms(x1, g_ref[...]).astype(BF16)

    def proj(c0):
        return _dot(h, wi_ref[:, c0:c0 + C_WIDTH])

    q_ref[0] = (proj(0) * (HEAD_DIM ** -0.5 * LOG2E)).astype(BF16)
    k_ref[0] = proj(C_WIDTH).astype(BF16)
    v = proj(2 * C_WIDTH)
    for j in range(tm // blk):
        vt_ref[0, j] = v[j * blk:(j + 1) * blk, :].T.astype(BF16)
    gc_ref[0] = _silu(proj(3 * C_WIDTH)).astype(BF16)


def _stick_kernel(q_ref, k_ref, vt_ref, gate_ref, tri_ref, o_ref, acc_ref, c_ref, qz_ref, *, blk, qm, groups):
    i = pl.program_id(2)
    qblk = qm * blk
    n = 2 * qblk
    pairs = range(groups)
    lanes = [slice(g * LANES, (g + 1) * LANES) for g in pairs]
    items = [(g, ct) for g in pairs for ct in range(n // MXU_COLS)]

    for g in pairs:
        qz_ref[g] = _split_halves(q_ref[0, :, lanes[g]])
        acc_ref[g] = jnp.zeros((LANES, n), F32)
        c_ref[g] = jnp.zeros((1, n), F32)

    def softplus2(z):
        return jnp.maximum(z, 0.0) + jnp.log(1.0 + jnp.exp2(-jnp.abs(z))) * LOG2E

    def step(kb, kinds):
        def qk(k):
            g, ct = items[k]
            return _dot_nt(k_ref[0, kb, :, lanes[g]], qz_ref[g, _tile(ct), :])

        pending = []

        def finish():
            k, z, s = pending.pop()
            g, ct = items[k]
            a = jnp.exp2(z + s[:blk])
            if kinds[ct] is not None:
                a = jnp.where(kinds[ct], a, 0.0)
            pv = _dot(vt_ref[0, kb, lanes[g], :], a.astype(BF16))
            carry = c_ref[g, :, _tile(ct)]
            acc_ref[g, :, _tile(ct)] += pv * jnp.exp2(carry)
            c_ref[g, :, _tile(ct)] = carry + s[blk:blk + 1]

        def process(k, z):
            ct = items[k][1]
            p = softplus2(z)
            if kinds[ct] is not None:
                p = jnp.where(kinds[ct], p, 0.0)
            s = _dot(tri_ref[...], p.astype(BF16))
            if pending:
                finish()
            pending.append((k, z, s))

        _pipeline(items, kinds, qk, process)
        finish()

    for d in reversed(range(qm)):
        kinds = {}
        for ct in range(n // MXU_COLS):
            ka, ql, q_lo, q_hi = _tile_positions(blk, qblk, d, ct)
            if d * blk >= q_hi:
                kinds[ct] = SKIP
            elif d * blk + blk - 1 < q_lo:
                kinds[ct] = None
            else:
                kinds[ct] = ka < ql
        step(i * qm + d, kinds)

    all_open = {ct: None for ct in range(n // MXU_COLS)}

    def live(j):
        return jnp.logical_and(j < i * qm, jnp.max(c_ref[...]) > EXP2_UNDERFLOW)

    def body(j):
        step(i * qm - 1 - j, all_open)
        return j + 1

    lax.while_loop(live, body, jnp.int32(0))

    for g in pairs:
        acc = acc_ref[g]
        o = jnp.concatenate([acc[:HEAD_DIM, :qblk], acc[HEAD_DIM:, qblk:]], axis=0).T
        o_ref[0, :, lanes[g]] = (o * gate_ref[0, :, lanes[g]].astype(F32)).astype(BF16)


def _out_kernel(x_ref, m_ref, w_ref, o_ref):
    o_ref[0] = x_ref[0] + _dot(m_ref[0], w_ref[...])


def _params(n_axes):
    return pltpu.CompilerParams(dimension_semantics=("arbitrary",) * n_axes,
                                vmem_limit_bytes=VMEM_LIMIT_BYTES)


def _const_spec(shape):
    nd = len(shape)
    return pl.BlockSpec(shape, lambda *_: (0,) * nd)


def _attn_specs(nb, blk, qblk, groups):
    w = groups * LANES
    qspec = pl.BlockSpec((1, qblk, w), lambda b, h, i: (b, i, h))
    kseq = pl.BlockSpec((1, nb, blk, w), lambda b, h, i: (b, 0, 0, h))
    vtseq = pl.BlockSpec((1, nb, w, blk), lambda b, h, i: (b, 0, h, 0))
    return qspec, kseq, vtseq


def kernel(x, norm_g, w_in_even, q_norm_g, k_norm_g, lam, subln_g, sgu_norm_g, w_s, b_s,
           w_out_even, w_in_odd, w_out_odd):
    bsz, seq, d = x.shape
    tm, blk, qm = _tile_sizes(seq)
    qblk = qm * blk
    nt, nb, nq = seq // tm, seq // blk, seq // qblk
    sds = jax.ShapeDtypeStruct

    reps = A_WIDTH // HEAD_DIM
    qg = (jnp.tile(q_norm_g[0], reps) * (HEAD_DIM ** -0.5 * LOG2E)).reshape(1, A_WIDTH)
    kg = jnp.tile(k_norm_g[0], reps).reshape(1, A_WIDTH)
    grp = jnp.arange(A_WIDTH) // HEAD_DIM
    gmat = jnp.where(grp[:, None] == grp[None, :], 1.0 / HEAD_DIM, 0.0).astype(BF16)
    pos = jnp.arange(B_CHUNK)
    ws = jnp.where((pos[:, None] // CHUNK) >= (pos[None, :] // CHUNK), w_s[0], 0.0).astype(BF16)
    bs_full = jnp.repeat(b_s[0].T, B_WIDTH // B_GROUPS, axis=1)
    slopes2 = jnp.exp2(-8.0 * jnp.arange(1, A_HEADS + 1, dtype=F32) / A_HEADS) * LOG2E
    rmax = (NORM_MARGIN * HEAD_DIM * jnp.max(jnp.abs(qg)) * jnp.max(jnp.abs(kg))).reshape(1)
    kp = jnp.arange(blk)
    tri = jnp.concatenate([jnp.where(kp[None, :] >= kp[:, None], -1.0, 0.0),
                           jnp.full((BF16_ROWS, blk), -1.0)], axis=0).astype(BF16)

    row_spec = lambda w: pl.BlockSpec((1, tm, w), lambda b, t: (b, t, 0))
    vt_out_spec = lambda w: pl.BlockSpec((1, tm // blk, w, blk), lambda b, t: (b, t, 0, 0))

    q0, k0, vt0, ga0, mb0 = pl.pallas_call(
        functools.partial(_even_in_kernel, tm=tm, blk=blk),
        grid=(bsz, nt),
        in_specs=[row_spec(d), _const_spec((1, d)), _const_spec(w_in_even.shape[1:]),
                  _const_spec((1, A_WIDTH)), _const_spec((1, A_WIDTH)), _const_spec((A_WIDTH, A_WIDTH)),
                  _const_spec((1, B_WIDTH)), _const_spec(ws.shape), _const_spec(bs_full.shape)],
        out_specs=[row_spec(A_WIDTH), row_spec(A_WIDTH), vt_out_spec(A_WIDTH), row_spec(A_WIDTH),
                   row_spec(B_WIDTH)],
        out_shape=[sds((bsz, seq, A_WIDTH), BF16), sds((bsz, seq, A_WIDTH), BF16),
                   sds((bsz, nb, A_WIDTH, blk), BF16), sds((bsz, seq, A_WIDTH), BF16),
                   sds((bsz, seq, B_WIDTH), BF16)],
        compiler_params=_params(2), name="even_in",
    )(x, norm_g[0:1], w_in_even[0].astype(BF16), qg, kg, gmat, sgu_norm_g[0:1], ws, bs_full)

    q_spec, kseq_spec, vtseq_spec = _attn_specs(nb, blk, qblk, DIFF_GROUPS)
    ma0 = pl.pallas_call(
        functools.partial(_diff_attn_kernel, blk=blk, qm=qm, groups=DIFF_GROUPS),
        grid=(bsz, A_HEADS // DIFF_GROUPS, nq),
        in_specs=[pl.BlockSpec(memory_space=pltpu.SMEM), pl.BlockSpec(memory_space=pltpu.SMEM),
                  q_spec, kseq_spec, vtseq_spec, q_spec, _const_spec(lam.shape[1:]), _const_spec((1, LANES))],
        out_specs=q_spec,
        out_shape=sds((bsz, seq, A_WIDTH), BF16),
        scratch_shapes=[pltpu.VMEM((DIFF_GROUPS, LANES, 2 * qblk), F32),
                        pltpu.VMEM((DIFF_GROUPS, blk, LANES), F32)],
        compiler_params=_params(3), name="diff_attn",
    )(slopes2, rmax, q0, k0.reshape(bsz, nb, blk, A_WIDTH), vt0, ga0, lam[0], subln_g[0:1])

    x1, q1, k1, vt1, gc1 = pl.pallas_call(
        functools.partial(_mid_kernel, tm=tm, blk=blk),
        grid=(bsz, nt),
        in_specs=[row_spec(d), row_spec(A_WIDTH), row_spec(B_WIDTH), _const_spec((d, d)),
                  _const_spec((1, d)), _const_spec(w_in_odd.shape[1:])],
        out_specs=[row_spec(d), row_spec(C_WIDTH), row_spec(C_WIDTH), vt_out_spec(C_WIDTH), row_spec(C_WIDTH)],
        out_shape=[sds((bsz, seq, d), F32), sds((bsz, seq, C_WIDTH), BF16), sds((bsz, seq, C_WIDTH), BF16),
                   sds((bsz, nb, C_WIDTH, blk), BF16), sds((bsz, seq, C_WIDTH), BF16)],
        compiler_params=_params(2), name="mid",
    )(x, ma0, mb0, w_out_even[0].astype(BF16), norm_g[1:2], w_in_odd[0].astype(BF16))

    q_spec, kseq_spec, vtseq_spec = _attn_specs(nb, blk, qblk, STICK_GROUPS)
    mc1 = pl.pallas_call(
        functools.partial(_stick_kernel, blk=blk, qm=qm, groups=STICK_GROUPS),
        grid=(bsz, C_WIDTH // LANES // STICK_GROUPS, nq),
        in_specs=[q_spec, kseq_spec, vtseq_spec, q_spec, _const_spec(tri.shape)],
        out_specs=q_spec,
        out_shape=sds((bsz, seq, C_WIDTH), BF16),
        scratch_shapes=[pltpu.VMEM((STICK_GROUPS, LANES, 2 * qblk), F32),
                        pltpu.VMEM((STICK_GROUPS, 1, 2 * qblk), F32),
                        pltpu.VMEM((STICK_GROUPS, 2 * qblk, LANES), BF16)],
        compiler_params=_params(3), name="stick_attn",
    )(q1, k1.reshape(bsz, nb, blk, C_WIDTH), vt1, gc1, tri)

    return pl.pallas_call(
        _out_kernel,
        grid=(bsz, nt),
        in_specs=[row_spec(d), row_spec(C_WIDTH), _const_spec((C_WIDTH, d))],
        out_specs=row_spec(d),
        out_shape=sds((bsz, seq, d), F32),
        compiler_params=_params(2), name="out_proj",
    )(x1, mc1, w_out_odd[0].astype(BF16))
```

```python
import functools
import math

import jax
import jax.numpy as jnp
from jax import lax
from jax.experimental import pallas as pl
from jax.experimental.pallas import tpu as pltpu

EPS = 1e-6
LOG2E = 1.4426950408889634
CHUNK = 64
HEAD_DIM = 64
LANES = 128
MXU_COLS = 256
BF16_ROWS = 16
A_HEADS = 4
A_WIDTH = 512
B_WIDTH = 512
B_CHUNK = 128
B_GROUPS = 4
C_WIDTH = 1024
LAMBDA_INIT_0 = 0.8 - 0.6 * math.exp(-0.3 * 0)
NEG_BIG = -1e30
EXP2_UNDERFLOW = -160.0
FROZEN_MAX_LIMIT = 40.0
NORM_MARGIN = 1.02
VMEM_LIMIT_BYTES = 56 * 1024 * 1024
DIFF_GROUPS = 4
STICK_GROUPS = 4
LOOKAHEAD = 2
SKIP = "skip"

F32 = jnp.float32
BF16 = jnp.bfloat16


def _tile_sizes(seq):
    blk = 256 if seq % 256 == 0 else 128
    tm = 512 if seq % 512 == 0 else blk
    qm = 2 if seq % (2 * blk) == 0 else 1
    return tm, blk, qm


def _dot(a, b):
    return jnp.dot(a, b, preferred_element_type=F32)


def _dot_nt(a, b):
    return lax.dot_general(a, b, (((1,), (1,)), ((), ())), preferred_element_type=F32)


def _silu(t):
    return t * (1.0 / (1.0 + jnp.exp(-t)))


def _rms(t, gain):
    ms = jnp.mean(t * t, axis=-1, keepdims=True)
    return t * lax.rsqrt(ms + EPS) * gain


def _split_halves(q):
    lane = lax.broadcasted_iota(jnp.int32, q.shape, 1)
    zero = jnp.zeros_like(q)
    return jnp.concatenate([jnp.where(lane < HEAD_DIM, q, zero), jnp.where(lane >= HEAD_DIM, q, zero)], axis=0)


def _tile(ct):
    return slice(ct * MXU_COLS, (ct + 1) * MXU_COLS)


def _tile_positions(blk, qblk, d, ct):
    ka = lax.broadcasted_iota(jnp.int32, (blk, MXU_COLS), 0) + d * blk
    ql = (lax.broadcasted_iota(jnp.int32, (blk, MXU_COLS), 1) + ct * MXU_COLS) & (qblk - 1)
    q_lo = (ct * MXU_COLS) % qblk
    return ka, ql, q_lo, q_lo + MXU_COLS - 1


def _pipeline(items, kinds, qk, process):
    active = [k for k in range(len(items)) if not isinstance(kinds[items[k][1]], str)]
    raw = {k: qk(k) for k in active[:LOOKAHEAD]}
    for idx, k in enumerate(active):
        if idx + LOOKAHEAD < len(active):
            nxt = active[idx + LOOKAHEAD]
            raw[nxt] = qk(nxt)
        process(k, raw.pop(k))


def _even_in_kernel(x_ref, g_ref, w_ref, qg_ref, kg_ref, gmat_ref, sg_ref, ws_ref, bs_ref,
                    q_ref, k_ref, vt_ref, ga_ref, mb_ref, *, tm, blk):
    h = _rms(x_ref[0], g_ref[...]).astype(BF16)

    def proj(c0, width):
        return _dot(h, w_ref[:, c0:c0 + width])

    def head_norm(t, gain_ref):
        sq = t * t
        hi = sq.astype(BF16)
        lo = (sq - hi.astype(F32)).astype(BF16)
        ms = _dot(hi, gmat_ref[...]) + _dot(lo, gmat_ref[...])
        return t * lax.rsqrt(ms + EPS) * gain_ref[...]

    q_ref[0] = head_norm(proj(0, A_WIDTH), qg_ref).astype(BF16)
    k_ref[0] = head_norm(proj(A_WIDTH, A_WIDTH), kg_ref).astype(BF16)
    va = proj(2 * A_WIDTH, A_WIDTH)
    for j in range(tm // blk):
        vt_ref[0, j] = va[j * blk:(j + 1) * blk, :].T.astype(BF16)
    ga_ref[0] = _silu(proj(3 * A_WIDTH, A_WIDTH)).astype(BF16)

    base = 4 * A_WIDTH
    u = jax.nn.gelu(proj(base, B_WIDTH))
    vn = _rms(jax.nn.gelu(proj(base + B_WIDTH, B_WIDTH)), sg_ref[...]).astype(BF16)
    sgate = _silu(proj(base + 2 * B_WIDTH, B_WIDTH))
    gd = B_WIDTH // B_GROUPS
    for c in range(tm // B_CHUNK):
        rows = slice(c * B_CHUNK, (c + 1) * B_CHUNK)
        for g in range(B_GROUPS):
            cols = slice(g * gd, (g + 1) * gd)
            mixed = _dot(ws_ref[g], vn[rows, cols]) + bs_ref[:, cols]
            mb_ref[0, rows, cols] = (u[rows, cols] * mixed * sgate[rows, cols]).astype(BF16)


def _diff_attn_kernel(sl_ref, rmax_ref, q_ref, k_ref, vt_ref, gate_ref, lam_ref, sub_ref, o_ref, acc_ref, kb_ref,
                      *, blk, qm, groups):
    hg = pl.program_id(1)
    i = pl.program_id(2)
    qblk = qm * blk
    n = 2 * qblk
    heads = range(groups)
    lanes = [slice(g * LANES, (g + 1) * LANES) for g in heads]
    slope2 = [sl_ref[hg * groups + g] for g in heads]
    qz = [_split_halves(q_ref[0, :, lanes[g]]) for g in heads]

    def update(hs, ms, ls, e, shift, kb, frozen=False):
        if frozen:
            p = {g: jnp.exp2(e[g] - (ms[g] - shift[g])) for g in hs}
            pv = {g: _dot(vt_ref[0, kb, lanes[g], :], p[g].astype(BF16)) for g in hs}
            for g in hs:
                acc_ref[g] += pv[g]
            l_new = {g: ls[g] + jnp.sum(p[g], axis=0, keepdims=True) for g in hs}
            return ms, tuple(l_new.get(g, ls[g]) for g in heads)
        m_new = {g: jnp.maximum(ms[g], jnp.max(e[g], axis=0, keepdims=True) + shift[g]) for g in hs}
        p = {g: jnp.exp2(e[g] - (m_new[g] - shift[g])) for g in hs}
        pv = {g: _dot(vt_ref[0, kb, lanes[g], :], p[g].astype(BF16)) for g in hs}
        alpha = {g: jnp.exp2(ms[g] - m_new[g]) for g in hs}
        for g in hs:
            acc_ref[g] = acc_ref[g] * alpha[g] + pv[g]
        l_new = {g: alpha[g] * ls[g] + jnp.sum(p[g], axis=0, keepdims=True) for g in hs}
        return tuple(m_new.get(g, ms[g]) for g in heads), tuple(l_new.get(g, ls[g]) for g in heads)

    krow = lax.broadcasted_iota(jnp.int32, (blk, LANES), 0).astype(F32)
    for g in heads:
        acc_ref[g] = jnp.zeros((LANES, n), F32)
        kb_ref[g] = slope2[g] * krow
    state = (tuple(jnp.full((1, n), NEG_BIG, F32) for _ in heads), tuple(jnp.zeros((1, n), F32) for _ in heads))

    def update_cols(ms, ls, e, kb, cols):
        w = cols[0].stop - cols[0].start
        take = lambda x: jnp.concatenate([x[:, c] for c in cols], axis=1)
        put = lambda full, part: jnp.concatenate(
            [full[:, :cols[0].start], part[:, :w], full[:, cols[0].stop:cols[1].start], part[:, w:]], axis=1)
        m_old = [take(ms[g]) for g in heads]
        m_new = [jnp.maximum(m_old[g], jnp.max(e[g], axis=0, keepdims=True)) for g in heads]
        p = [jnp.exp2(e[g] - m_new[g]) for g in heads]
        pv = [_dot(vt_ref[0, kb, lanes[g], :], p[g].astype(BF16)) for g in heads]
        alpha = [jnp.exp2(m_old[g] - m_new[g]) for g in heads]
        for g in heads:
            for half, c in enumerate(cols):
                part = slice(half * w, (half + 1) * w)
                acc_ref[g, :, c] = acc_ref[g, :, c] * alpha[g][:, part] + pv[g][:, part]
        l_new = [alpha[g] * take(ls[g]) + jnp.sum(p[g], axis=0, keepdims=True) for g in heads]
        return tuple(put(ms[g], m_new[g]) for g in heads), tuple(put(ls[g], l_new[g]) for g in heads)

    for d in range(qm):
        lo = d * blk
        w = qblk - lo
        cols = [slice(lo, qblk), slice(qblk + lo, n)]
        lane = lax.broadcasted_iota(jnp.int32, (blk, 2 * w), 1)
        ql = lo + jnp.where(lane < w, lane, lane - w)
        ka = lax.broadcasted_iota(jnp.int32, (blk, 2 * w), 0) + lo
        allowed = (ka // CHUNK) <= (ql // CHUNK)
        rel = (ql - jnp.abs(ka - ql)).astype(F32)
        kb = i * qm + d
        qd = [qz[g] if d == 0 else jnp.concatenate([qz[g][c] for c in cols], axis=0) for g in heads]
        raw = [_dot_nt(k_ref[0, kb, :, lanes[g]], qd[g]) for g in heads]
        e = [jnp.where(allowed, raw[g] + slope2[g] * rel, NEG_BIG) for g in heads]
        state = update(heads, *state, e, [0.0] * groups, kb) if d == 0 else update_cols(*state, e, kb, cols)

    total = i * qm
    rmax = rmax_ref[0]

    def walk(hs, watched, state, frozen):
        def reach(j, mmin):
            far = (j * blk).astype(F32)
            alive = [rmax - slope2[g] * far - mmin[g] > EXP2_UNDERFLOW for g in watched]
            return jnp.logical_and(j < total, functools.reduce(jnp.logical_or, alive))

        def body(carry):
            j, ms, ls, mmin = carry
            kb = total - 1 - j
            off = (kb * blk - i * qblk).astype(F32)
            raw = {g: _dot_nt(k_ref[0, kb, :, lanes[g]], qz[g]) for g in hs}
            e = {g: raw[g] + jnp.concatenate([kb_ref[g]] * (n // LANES), axis=1) for g in hs}
            if not frozen:
                mmin = tuple(jnp.min(ms[g]) for g in heads)
            ms, ls = update(hs, ms, ls, e, {g: slope2[g] * off for g in hs}, kb, frozen)
            return j + 1, ms, ls, mmin

        j, ms, ls = state
        mmin = tuple(jnp.min(ms[g]) for g in heads)
        j, ms, ls, _ = lax.while_loop(lambda c: reach(c[0], c[3]), body, (j, ms, ls, mmin))
        return j, ms, ls

    hlist = list(heads)
    phases = [(hlist[k:], [k]) for k in range(groups - 2)] + [(hlist[groups - 2:], hlist[groups - 2:])]

    def walks(frozen):
        def run():
            st = (jnp.int32(0),) + state
            for hs, watched in phases:
                st = walk(hs, watched, st, frozen)
            return st[2]
        return run

    ls = lax.cond(rmax <= FROZEN_MAX_LIMIT, walks(True), walks(False))

    lam = lam_ref[...]
    lam_full = (jnp.exp(jnp.sum(lam[0:1] * lam[1:2], axis=1, keepdims=True))
                - jnp.exp(jnp.sum(lam[2:3] * lam[3:4], axis=1, keepdims=True)) + LAMBDA_INIT_0)
    for g in heads:
        acc = acc_ref[g] * (1.0 / ls[g])
        o = (acc[:, :qblk] - lam_full * acc[:, qblk:]).T
        o = _rms(o, sub_ref[...]) * (1.0 - LAMBDA_INIT_0)
        o_ref[0, :, lanes[g]] = (o * gate_ref[0, :, lanes[g]].astype(F32)).astype(BF16)


def _mid_kernel(x_ref, ma_ref, mb_ref, wo_ref, g_ref, wi_ref,
                x1_ref, q_ref, k_ref, vt_ref, gc_ref, *, tm, blk):
    x1 = x_ref[0] + _dot(ma_ref[0], wo_ref[0:A_WIDTH, :]) + _dot(mb_ref[0], wo_ref[A_WIDTH:, :])
    x1_ref[0] = x1
    h = _rms(x1, g_ref[...]).astype(BF16)

    def proj(c0):
        return _dot(h, wi_ref[:, c0:c0 + C_WIDTH])

    q_ref[0] = (proj(0) * (HEAD_DIM ** -0.5 * LOG2E)).astype(BF16)
    k_ref[0] = proj(C_WIDTH).astype(BF16)
    v = proj(2 * C_WIDTH)
    for j in range(tm // blk):
        vt_ref[0, j] = v[j * blk:(j + 1) * blk, :].T.astype(BF16)
    gc_ref[0] = _silu(proj(3 * C_WIDTH)).astype(BF16)


def _stick_kernel(q_ref, k_ref, vt_ref, gate_ref, tri_ref, o_ref, acc_ref, c_ref, qz_ref, *, blk, qm, groups):
    i = pl.program_id(2)
    qblk = qm * blk
    n = 2 * qblk
    pairs = range(groups)
    lanes = [slice(g * LANES, (g + 1) * LANES) for g in pairs]
    items = [(g, ct) for g in pairs for ct in range(n // MXU_COLS)]

    for g in pairs:
        qz_ref[g] = _split_halves(q_ref[0, :, lanes[g]])
        acc_ref[g] = jnp.zeros((LANES, n), F32)
        c_ref[g] = jnp.zeros((1, n), F32)

    def softplus2(z):
        return jnp.maximum(z, 0.0) + jnp.log(1.0 + jnp.exp2(-jnp.abs(z))) * LOG2E

    def step(kb, kinds):
        def qk(k):
            g, ct = items[k]
            return _dot_nt(k_ref[0, kb, :, lanes[g]], qz_ref[g, _tile(ct), :])

        pending = []

        def finish():
            k, z, s = pending.pop()
            g, ct = items[k]
            a = jnp.exp2(z + s[:blk])
            if kinds[ct] is not None:
                a = jnp.where(kinds[ct], a, 0.0)
            pv = _dot(vt_ref[0, kb, lanes[g], :], a.astype(BF16))
            carry = c_ref[g, :, _tile(ct)]
            acc_ref[g, :, _tile(ct)] += pv * jnp.exp2(carry)
            c_ref[g, :, _tile(ct)] = carry + s[blk:blk + 1]

        def process(k, z):
            ct = items[k][1]
            p = softplus2(z)
            if kinds[ct] is not None:
                p = jnp.where(kinds[ct], p, 0.0)
            s = _dot(tri_ref[...], p.astype(BF16))
            if pending:
                finish()
            pending.append((k, z, s))

        _pipeline(items, kinds, qk, process)
        finish()

    for d in reversed(range(qm)):
        kinds = {}
        for ct in range(n // MXU_COLS):
            ka, ql, q_lo, q_hi = _tile_positions(blk, qblk, d, ct)
            if d * blk >= q_hi:
                kinds[ct] = SKIP
            elif d * blk + blk - 1 < q_lo:
                kinds[ct] = None
            else:
                kinds[ct] = ka < ql
        step(i * qm + d, kinds)

    tiles = range(n // MXU_COLS)
    all_open = {ct: None for ct in tiles}
    first_only = {ct: (None if (ct * MXU_COLS) % qblk == 0 else SKIP) for ct in tiles}
    later = [_tile(ct) for ct in tiles if first_only[ct] is SKIP]

    def reach():
        alive = jnp.max(c_ref[...]) > EXP2_UNDERFLOW
        if not later:
            return alive, alive
        return alive, functools.reduce(jnp.logical_or, [jnp.max(c_ref[:, :, t]) > EXP2_UNDERFLOW for t in later])

    def body(carry):
        j, _, later_alive = carry
        kb = i * qm - 1 - j
        if later:
            lax.cond(later_alive, lambda: (step(kb, all_open), 0)[1], lambda: (step(kb, first_only), 0)[1])
        else:
            step(kb, all_open)
        return (j + 1,) + reach()

    lax.while_loop(lambda c: jnp.logical_and(c[0] < i * qm, c[1]), body, (jnp.int32(0),) + reach())

    for g in pairs:
        acc = acc_ref[g]
        o = jnp.concatenate([acc[:HEAD_DIM, :qblk], acc[HEAD_DIM:, qblk:]], axis=0).T
        o_ref[0, :, lanes[g]] = (o * gate_ref[0, :, lanes[g]].astype(F32)).astype(BF16)


def _out_kernel(x_ref, m_ref, w_ref, o_ref):
    o_ref[0] = x_ref[0] + _dot(m_ref[0], w_ref[...])


def _params(n_axes):
    return pltpu.CompilerParams(dimension_semantics=("arbitrary",) * n_axes,
                                vmem_limit_bytes=VMEM_LIMIT_BYTES)


def _const_spec(shape):
    nd = len(shape)
    return pl.BlockSpec(shape, lambda *_: (0,) * nd)


def _attn_specs(nb, blk, qblk, groups):
    w = groups * LANES
    qspec = pl.BlockSpec((1, qblk, w), lambda b, h, i: (b, i, h))
    kseq = pl.BlockSpec((1, nb, blk, w), lambda b, h, i: (b, 0, 0, h))
    vtseq = pl.BlockSpec((1, nb, w, blk), lambda b, h, i: (b, 0, h, 0))
    return qspec, kseq, vtseq


def kernel(x, norm_g, w_in_even, q_norm_g, k_norm_g, lam, subln_g, sgu_norm_g, w_s, b_s,
           w_out_even, w_in_odd, w_out_odd):
    bsz, seq, d = x.shape
    tm, blk, qm = _tile_sizes(seq)
    qblk = qm * blk
    nt, nb, nq = seq // tm, seq // blk, seq // qblk
    sds = jax.ShapeDtypeStruct

    reps = A_WIDTH // HEAD_DIM
    qg = (jnp.tile(q_norm_g[0], reps) * (HEAD_DIM ** -0.5 * LOG2E)).reshape(1, A_WIDTH)
    kg = jnp.tile(k_norm_g[0], reps).reshape(1, A_WIDTH)
    grp = jnp.arange(A_WIDTH) // HEAD_DIM
    gmat = jnp.where(grp[:, None] == grp[None, :], 1.0 / HEAD_DIM, 0.0).astype(BF16)
    pos = jnp.arange(B_CHUNK)
    ws = jnp.where((pos[:, None] // CHUNK) >= (pos[None, :] // CHUNK), w_s[0], 0.0).astype(BF16)
    bs_full = jnp.repeat(b_s[0].T, B_WIDTH // B_GROUPS, axis=1)
    slopes2 = jnp.exp2(-8.0 * jnp.arange(1, A_HEADS + 1, dtype=F32) / A_HEADS) * LOG2E
    rmax = (NORM_MARGIN * HEAD_DIM * jnp.max(jnp.abs(qg)) * jnp.max(jnp.abs(kg))).reshape(1)
    kp = jnp.arange(blk)
    tri = jnp.concatenate([jnp.where(kp[None, :] >= kp[:, None], -1.0, 0.0),
                           jnp.full((BF16_ROWS, blk), -1.0)], axis=0).astype(BF16)

    row_spec = lambda w: pl.BlockSpec((1, tm, w), lambda b, t: (b, t, 0))
    vt_out_spec = lambda w: pl.BlockSpec((1, tm // blk, w, blk), lambda b, t: (b, t, 0, 0))

    q0, k0, vt0, ga0, mb0 = pl.pallas_call(
        functools.partial(_even_in_kernel, tm=tm, blk=blk),
        grid=(bsz, nt),
        in_specs=[row_spec(d), _const_spec((1, d)), _const_spec(w_in_even.shape[1:]),
                  _const_spec((1, A_WIDTH)), _const_spec((1, A_WIDTH)), _const_spec((A_WIDTH, A_WIDTH)),
                  _const_spec((1, B_WIDTH)), _const_spec(ws.shape), _const_spec(bs_full.shape)],
        out_specs=[row_spec(A_WIDTH), row_spec(A_WIDTH), vt_out_spec(A_WIDTH), row_spec(A_WIDTH),
                   row_spec(B_WIDTH)],
        out_shape=[sds((bsz, seq, A_WIDTH), BF16), sds((bsz, seq, A_WIDTH), BF16),
                   sds((bsz, nb, A_WIDTH, blk), BF16), sds((bsz, seq, A_WIDTH), BF16),
                   sds((bsz, seq, B_WIDTH), BF16)],
        compiler_params=_params(2), name="even_in",
    )(x, norm_g[0:1], w_in_even[0].astype(BF16), qg, kg, gmat, sgu_norm_g[0:1], ws, bs_full)

    q_spec, kseq_spec, vtseq_spec = _attn_specs(nb, blk, qblk, DIFF_GROUPS)
    ma0 = pl.pallas_call(
        functools.partial(_diff_attn_kernel, blk=blk, qm=qm, groups=DIFF_GROUPS),
        grid=(bsz, A_HEADS // DIFF_GROUPS, nq),
        in_specs=[pl.BlockSpec(memory_space=pltpu.SMEM), pl.BlockSpec(memory_space=pltpu.SMEM),
                  q_spec, kseq_spec, vtseq_spec, q_spec, _const_spec(lam.shape[1:]), _const_spec((1, LANES))],
        out_specs=q_spec,
        out_shape=sds((bsz, seq, A_WIDTH), BF16),
        scratch_shapes=[pltpu.VMEM((DIFF_GROUPS, LANES, 2 * qblk), F32),
                        pltpu.VMEM((DIFF_GROUPS, blk, LANES), F32)],
        compiler_params=_params(3), name="diff_attn",
    )(slopes2, rmax, q0, k0.reshape(bsz, nb, blk, A_WIDTH), vt0, ga0, lam[0], subln_g[0:1])

    x1, q1, k1, vt1, gc1 = pl.pallas_call(
        functools.partial(_mid_kernel, tm=tm, blk=blk),
        grid=(bsz, nt),
        in_specs=[row_spec(d), row_spec(A_WIDTH), row_spec(B_WIDTH), _const_spec((d, d)),
                  _const_spec((1, d)), _const_spec(w_in_odd.shape[1:])],
        out_specs=[row_spec(d), row_spec(C_WIDTH), row_spec(C_WIDTH), vt_out_spec(C_WIDTH), row_spec(C_WIDTH)],
        out_shape=[sds((bsz, seq, d), F32), sds((bsz, seq, C_WIDTH), BF16), sds((bsz, seq, C_WIDTH), BF16),
                   sds((bsz, nb, C_WIDTH, blk), BF16), sds((bsz, seq, C_WIDTH), BF16)],
        compiler_params=_params(2), name="mid",
    )(x, ma0, mb0, w_out_even[0].astype(BF16), norm_g[1:2], w_in_odd[0].astype(BF16))

    q_spec, kseq_spec, vtseq_spec = _attn_specs(nb, blk, qblk, STICK_GROUPS)
    mc1 = pl.pallas_call(
        functools.partial(_stick_kernel, blk=blk, qm=qm, groups=STICK_GROUPS),
        grid=(bsz, C_WIDTH // LANES // STICK_GROUPS, nq),
        in_specs=[q_spec, kseq_spec, vtseq_spec, q_spec, _const_spec(tri.shape)],
        out_specs=q_spec,
        out_shape=sds((bsz, seq, C_WIDTH), BF16),
        scratch_shapes=[pltpu.VMEM((STICK_GROUPS, LANES, 2 * qblk), F32),
                        pltpu.VMEM((STICK_GROUPS, 1, 2 * qblk), F32),
                        pltpu.VMEM((STICK_GROUPS, 2 * qblk, LANES), BF16)],
        compiler_params=_params(3), name="stick_attn",
    )(q1, k1.reshape(bsz, nb, blk, C_WIDTH), vt1, gc1, tri)

    return pl.pallas_call(
        _out_kernel,
        grid=(bsz, nt),
        in_specs=[row_spec(d), row_spec(C_WIDTH), _const_spec((C_WIDTH, d))],
        out_specs=row_spec(d),
        out_shape=sds((bsz, seq, d), F32),
        compiler_params=_params(2), name="out_proj",
    )(x1, mc1, w_out_odd[0].astype(BF16))
```

```python
import functools
import math

import jax
import jax.numpy as jnp
from jax import lax
from jax.experimental import pallas as pl
from jax.experimental.pallas import tpu as pltpu

EPS = 1e-6
LOG2E = 1.4426950408889634
CHUNK = 64
HEAD_DIM = 64
LANES = 128
MXU_COLS = 256
BF16_ROWS = 16
A_HEADS = 4
A_WIDTH = 512
B_WIDTH = 512
B_CHUNK = 128
B_GROUPS = 4
C_WIDTH = 1024
LAMBDA_INIT_0 = 0.8 - 0.6 * math.exp(-0.3 * 0)
NEG_BIG = -1e30
EXP2_UNDERFLOW = -160.0
FROZEN_MAX_LIMIT = 40.0
NORM_MARGIN = 1.02
VMEM_LIMIT_BYTES = 56 * 1024 * 1024
DIFF_GROUPS = 4
STICK_GROUPS = 4
LOOKAHEAD = 2
SKIP = "skip"

F32 = jnp.float32
BF16 = jnp.bfloat16


def _tile_sizes(seq):
    blk = 256 if seq % 256 == 0 else 128
    tm = 512 if seq % 512 == 0 else blk
    qm = 2 if seq % (2 * blk) == 0 else 1
    return tm, blk, qm


def _dot(a, b):
    return jnp.dot(a, b, preferred_element_type=F32)


def _dot_nt(a, b):
    return lax.dot_general(a, b, (((1,), (1,)), ((), ())), preferred_element_type=F32)


def _silu(t):
    return t * (1.0 / (1.0 + jnp.exp(-t)))


def _rms(t, gain):
    ms = jnp.mean(t * t, axis=-1, keepdims=True)
    return t * lax.rsqrt(ms + EPS) * gain


def _split_halves(q):
    lane = lax.broadcasted_iota(jnp.int32, q.shape, 1)
    zero = jnp.zeros_like(q)
    return jnp.concatenate([jnp.where(lane < HEAD_DIM, q, zero), jnp.where(lane >= HEAD_DIM, q, zero)], axis=0)


def _tile(ct):
    return slice(ct * MXU_COLS, (ct + 1) * MXU_COLS)


def _tile_positions(blk, qblk, d, ct):
    ka = lax.broadcasted_iota(jnp.int32, (blk, MXU_COLS), 0) + d * blk
    ql = (lax.broadcasted_iota(jnp.int32, (blk, MXU_COLS), 1) + ct * MXU_COLS) & (qblk - 1)
    q_lo = (ct * MXU_COLS) % qblk
    return ka, ql, q_lo, q_lo + MXU_COLS - 1


def _pipeline(work, qk, process):
    raw = {k: qk(work[k]) for k in range(min(LOOKAHEAD, len(work)))}
    for k, item in enumerate(work):
        if k + LOOKAHEAD < len(work):
            raw[k + LOOKAHEAD] = qk(work[k + LOOKAHEAD])
        process(item, raw.pop(k))


def _even_in_kernel(x_ref, g_ref, w_ref, qg_ref, kg_ref, gmat_ref, sg_ref, ws_ref, bs_ref,
                    q_ref, k_ref, vt_ref, ga_ref, mb_ref, *, tm, blk):
    h = _rms(x_ref[0], g_ref[...]).astype(BF16)

    def proj(c0, width):
        return _dot(h, w_ref[:, c0:c0 + width])

    def head_norm(t, gain_ref):
        sq = t * t
        hi = sq.astype(BF16)
        lo = (sq - hi.astype(F32)).astype(BF16)
        slabs = [slice(s, s + LANES) for s in range(0, t.shape[1], LANES)]
        ms = jnp.concatenate([_dot(hi[:, s], gmat_ref[...]) + _dot(lo[:, s], gmat_ref[...]) for s in slabs], axis=1)
        return t * lax.rsqrt(ms + EPS) * gain_ref[...]

    q_ref[0] = head_norm(proj(0, A_WIDTH), qg_ref).astype(BF16)
    k_ref[0] = head_norm(proj(A_WIDTH, A_WIDTH), kg_ref).astype(BF16)
    va = proj(2 * A_WIDTH, A_WIDTH)
    for j in range(tm // blk):
        vt_ref[0, j] = va[j * blk:(j + 1) * blk, :].T.astype(BF16)
    ga_ref[0] = _silu(proj(3 * A_WIDTH, A_WIDTH)).astype(BF16)

    base = 4 * A_WIDTH
    u = jax.nn.gelu(proj(base, B_WIDTH))
    vn = _rms(jax.nn.gelu(proj(base + B_WIDTH, B_WIDTH)), sg_ref[...]).astype(BF16)
    sgate = _silu(proj(base + 2 * B_WIDTH, B_WIDTH))
    gd = B_WIDTH // B_GROUPS
    for c in range(tm // B_CHUNK):
        rows = slice(c * B_CHUNK, (c + 1) * B_CHUNK)
        for g in range(B_GROUPS):
            cols = slice(g * gd, (g + 1) * gd)
            mixed = _dot(ws_ref[g], vn[rows, cols]) + bs_ref[:, cols]
            mb_ref[0, rows, cols] = (u[rows, cols] * mixed * sgate[rows, cols]).astype(BF16)


def _diff_attn_kernel(sl_ref, rmax_ref, q_ref, k_ref, vt_ref, gate_ref, lam_ref, sub_ref, o_ref, acc_ref, kb_ref,
                      *, blk, qm, groups):
    hg = pl.program_id(1)
    i = pl.program_id(2)
    qblk = qm * blk
    n = 2 * qblk
    heads = range(groups)
    lanes = [slice(g * LANES, (g + 1) * LANES) for g in heads]
    slope2 = [sl_ref[hg * groups + g] for g in heads]
    qz = [_split_halves(q_ref[0, :, lanes[g]]) for g in heads]

    def update(hs, ms, ls, e, shift, kb, frozen=False):
        if frozen:
            p = {g: jnp.exp2(e[g] - (ms[g] - shift[g])) for g in hs}
            pv = {g: _dot(vt_ref[0, kb, lanes[g], :], p[g].astype(BF16)) for g in hs}
            for g in hs:
                acc_ref[g] += pv[g]
            l_new = {g: ls[g] + jnp.sum(p[g], axis=0, keepdims=True) for g in hs}
            return ms, tuple(l_new.get(g, ls[g]) for g in heads)
        m_new = {g: jnp.maximum(ms[g], jnp.max(e[g], axis=0, keepdims=True) + shift[g]) for g in hs}
        p = {g: jnp.exp2(e[g] - (m_new[g] - shift[g])) for g in hs}
        pv = {g: _dot(vt_ref[0, kb, lanes[g], :], p[g].astype(BF16)) for g in hs}
        alpha = {g: jnp.exp2(ms[g] - m_new[g]) for g in hs}
        for g in hs:
            acc_ref[g] = acc_ref[g] * alpha[g] + pv[g]
        l_new = {g: alpha[g] * ls[g] + jnp.sum(p[g], axis=0, keepdims=True) for g in hs}
        return tuple(m_new.get(g, ms[g]) for g in heads), tuple(l_new.get(g, ls[g]) for g in heads)

    krow = lax.broadcasted_iota(jnp.int32, (blk, LANES), 0).astype(F32)
    for g in heads:
        acc_ref[g] = jnp.zeros((LANES, n), F32)
        kb_ref[g] = slope2[g] * krow
    state = (tuple(jnp.full((1, n), NEG_BIG, F32) for _ in heads), tuple(jnp.zeros((1, n), F32) for _ in heads))

    def update_cols(ms, ls, e, kb, cols):
        w = cols[0].stop - cols[0].start
        take = lambda x: jnp.concatenate([x[:, c] for c in cols], axis=1)
        put = lambda full, part: jnp.concatenate(
            [full[:, :cols[0].start], part[:, :w], full[:, cols[0].stop:cols[1].start], part[:, w:]], axis=1)
        m_old = [take(ms[g]) for g in heads]
        m_new = [jnp.maximum(m_old[g], jnp.max(e[g], axis=0, keepdims=True)) for g in heads]
        p = [jnp.exp2(e[g] - m_new[g]) for g in heads]
        pv = [_dot(vt_ref[0, kb, lanes[g], :], p[g].astype(BF16)) for g in heads]
        alpha = [jnp.exp2(m_old[g] - m_new[g]) for g in heads]
        for g in heads:
            for half, c in enumerate(cols):
                part = slice(half * w, (half + 1) * w)
                acc_ref[g, :, c] = acc_ref[g, :, c] * alpha[g][:, part] + pv[g][:, part]
        l_new = [alpha[g] * take(ls[g]) + jnp.sum(p[g], axis=0, keepdims=True) for g in heads]
        return tuple(put(ms[g], m_new[g]) for g in heads), tuple(put(ls[g], l_new[g]) for g in heads)

    for d in range(qm):
        lo = d * blk
        w = qblk - lo
        cols = [slice(lo, qblk), slice(qblk + lo, n)]
        lane = lax.broadcasted_iota(jnp.int32, (blk, 2 * w), 1)
        ql = lo + jnp.where(lane < w, lane, lane - w)
        ka = lax.broadcasted_iota(jnp.int32, (blk, 2 * w), 0) + lo
        allowed = (ka // CHUNK) <= (ql // CHUNK)
        rel = (ql - jnp.abs(ka - ql)).astype(F32)
        kb = i * qm + d
        qd = [qz[g] if d == 0 else jnp.concatenate([qz[g][c] for c in cols], axis=0) for g in heads]
        raw = [_dot_nt(k_ref[0, kb, :, lanes[g]], qd[g]) for g in heads]
        e = [jnp.where(allowed, raw[g] + slope2[g] * rel, NEG_BIG) for g in heads]
        state = update(heads, *state, e, [0.0] * groups, kb) if d == 0 else update_cols(*state, e, kb, cols)

    total = i * qm
    rmax = rmax_ref[0]

    def walk(hs, watched, state, frozen):
        def reach(j, mmin):
            far = (j * blk).astype(F32)
            alive = [rmax - slope2[g] * far - mmin[g] > EXP2_UNDERFLOW for g in watched]
            return jnp.logical_and(j < total, functools.reduce(jnp.logical_or, alive))

        def body(carry):
            j, ms, ls, mmin = carry
            kb = total - 1 - j
            off = (kb * blk - i * qblk).astype(F32)
            raw = {g: _dot_nt(k_ref[0, kb, :, lanes[g]], qz[g]) for g in hs}
            e = {g: raw[g] + jnp.concatenate([kb_ref[g]] * (n // LANES), axis=1) for g in hs}
            if not frozen:
                mmin = tuple(jnp.min(ms[g]) for g in heads)
            ms, ls = update(hs, ms, ls, e, {g: slope2[g] * off for g in hs}, kb, frozen)
            return j + 1, ms, ls, mmin

        j, ms, ls = state
        mmin = tuple(jnp.min(ms[g]) for g in heads)
        j, ms, ls, _ = lax.while_loop(lambda c: reach(c[0], c[3]), body, (j, ms, ls, mmin))
        return j, ms, ls

    hlist = list(heads)
    phases = [(hlist[k:], [k]) for k in range(groups - 2)] + [(hlist[groups - 2:], hlist[groups - 2:])]

    def walks(frozen):
        def run():
            st = (jnp.int32(0),) + state
            for hs, watched in phases:
                st = walk(hs, watched, st, frozen)
            return st[2]
        return run

    ls = lax.cond(rmax <= FROZEN_MAX_LIMIT, walks(True), walks(False))

    lam = lam_ref[...]
    lam_full = (jnp.exp(jnp.sum(lam[0:1] * lam[1:2], axis=1, keepdims=True))
                - jnp.exp(jnp.sum(lam[2:3] * lam[3:4], axis=1, keepdims=True)) + LAMBDA_INIT_0)
    for g in heads:
        acc = acc_ref[g] * (1.0 / ls[g])
        o = (acc[:, :qblk] - lam_full * acc[:, qblk:]).T
        o = _rms(o, sub_ref[...]) * (1.0 - LAMBDA_INIT_0)
        o_ref[0, :, lanes[g]] = (o * gate_ref[0, :, lanes[g]].astype(F32)).astype(BF16)


def _mid_kernel(x_ref, ma_ref, mb_ref, wo_ref, g_ref, wi_ref,
                x1_ref, q_ref, k_ref, vt_ref, gc_ref, *, tm, blk):
    x1 = x_ref[0] + _dot(ma_ref[0], wo_ref[0:A_WIDTH, :]) + _dot(mb_ref[0], wo_ref[A_WIDTH:, :])
    x1_ref[0] = x1
    h = _rms(x1, g_ref[...]).astype(BF16)

    def proj(c0):
        return _dot(h, wi_ref[:, c0:c0 + C_WIDTH])

    q_ref[0] = (proj(0) * (HEAD_DIM ** -0.5 * LOG2E)).astype(BF16)
    k_ref[0] = proj(C_WIDTH).astype(BF16)
    v = proj(2 * C_WIDTH)
    for j in range(tm // blk):
        vt_ref[0, j] = v[j * blk:(j + 1) * blk, :].T.astype(BF16)
    gc_ref[0] = _silu(proj(3 * C_WIDTH)).astype(BF16)


def _stick_kernel(q_ref, k_ref, vt_ref, gate_ref, tri_ref, o_ref, acc_ref, c_ref, qz_ref, *, blk, qm, groups):
    i = pl.program_id(2)
    qblk = qm * blk
    n = 2 * qblk
    pairs = range(groups)
    lanes = [slice(g * LANES, (g + 1) * LANES) for g in pairs]
    items = [(g, ct) for g in pairs for ct in range(n // MXU_COLS)]

    for g in pairs:
        qz_ref[g] = _split_halves(q_ref[0, :, lanes[g]])
        acc_ref[g] = jnp.zeros((LANES, n), F32)
        c_ref[g] = jnp.zeros((1, n), F32)

    def softplus2(z):
        return jnp.maximum(z, 0.0) + jnp.log(1.0 + jnp.exp2(-jnp.abs(z))) * LOG2E

    def step(blocks):
        work = [(kb, kinds[ct], g, ct) for kb, kinds in blocks for g, ct in items if kinds[ct] is not SKIP]

        def qk(item):
            kb, _, g, ct = item
            return _dot_nt(k_ref[0, kb, :, lanes[g]], qz_ref[g, _tile(ct), :])

        pending = []

        def finish():
            (kb, kind, g, ct), z, s = pending.pop()
            a = jnp.exp2(z + s[:blk])
            if kind is not None:
                a = jnp.where(kind, a, 0.0)
            pv = _dot(vt_ref[0, kb, lanes[g], :], a.astype(BF16))
            carry = c_ref[g, :, _tile(ct)]
            acc_ref[g, :, _tile(ct)] += pv * jnp.exp2(carry)
            c_ref[g, :, _tile(ct)] = carry + s[blk:blk + 1]

        def process(item, z):
            p = softplus2(z)
            if item[1] is not None:
                p = jnp.where(item[1], p, 0.0)
            s = _dot(tri_ref[...], p.astype(BF16))
            if pending:
                finish()
            pending.append((item, z, s))

        _pipeline(work, qk, process)
        finish()

    diagonal = []
    for d in reversed(range(qm)):
        kinds = {}
        for ct in range(n // MXU_COLS):
            ka, ql, q_lo, q_hi = _tile_positions(blk, qblk, d, ct)
            if d * blk >= q_hi:
                kinds[ct] = SKIP
            elif d * blk + blk - 1 < q_lo:
                kinds[ct] = None
            else:
                kinds[ct] = ka < ql
        diagonal.append((i * qm + d, kinds))
    step(diagonal)

    tiles = range(n // MXU_COLS)
    all_open = {ct: None for ct in tiles}
    first_only = {ct: (None if (ct * MXU_COLS) % qblk == 0 else SKIP) for ct in tiles}
    later = [_tile(ct) for ct in tiles if first_only[ct] is SKIP]

    def reach():
        alive = jnp.max(c_ref[...]) > EXP2_UNDERFLOW
        if not later:
            return alive, alive
        return alive, functools.reduce(jnp.logical_or, [jnp.max(c_ref[:, :, t]) > EXP2_UNDERFLOW for t in later])

    def body(carry):
        j, _, later_alive = carry
        kb = i * qm - 1 - j
        if later:
            lax.cond(later_alive, lambda: (step([(kb, all_open)]), 0)[1], lambda: (step([(kb, first_only)]), 0)[1])
        else:
            step([(kb, all_open)])
        return (j + 1,) + reach()

    lax.while_loop(lambda c: jnp.logical_and(c[0] < i * qm, c[1]), body, (jnp.int32(0),) + reach())

    for g in pairs:
        acc = acc_ref[g]
        o = jnp.concatenate([acc[:HEAD_DIM, :qblk], acc[HEAD_DIM:, qblk:]], axis=0).T
        o_ref[0, :, lanes[g]] = (o * gate_ref[0, :, lanes[g]].astype(F32)).astype(BF16)


def _out_kernel(x_ref, m_ref, w_ref, o_ref):
    o_ref[0] = x_ref[0] + _dot(m_ref[0], w_ref[...])


def _params(n_axes):
    return pltpu.CompilerParams(dimension_semantics=("arbitrary",) * n_axes,
                                vmem_limit_bytes=VMEM_LIMIT_BYTES)


def _const_spec(shape):
    nd = len(shape)
    return pl.BlockSpec(shape, lambda *_: (0,) * nd)


def _attn_specs(nb, blk, qblk, groups):
    w = groups * LANES
    qspec = pl.BlockSpec((1, qblk, w), lambda b, h, i: (b, i, h))
    kseq = pl.BlockSpec((1, nb, blk, w), lambda b, h, i: (b, 0, 0, h))
    vtseq = pl.BlockSpec((1, nb, w, blk), lambda b, h, i: (b, 0, h, 0))
    return qspec, kseq, vtseq


def kernel(x, norm_g, w_in_even, q_norm_g, k_norm_g, lam, subln_g, sgu_norm_g, w_s, b_s,
           w_out_even, w_in_odd, w_out_odd):
    bsz, seq, d = x.shape
    tm, blk, qm = _tile_sizes(seq)
    qblk = qm * blk
    nt, nb, nq = seq // tm, seq // blk, seq // qblk
    sds = jax.ShapeDtypeStruct

    reps = A_WIDTH // HEAD_DIM
    qg = (jnp.tile(q_norm_g[0], reps) * (HEAD_DIM ** -0.5 * LOG2E)).reshape(1, A_WIDTH)
    kg = jnp.tile(k_norm_g[0], reps).reshape(1, A_WIDTH)
    grp = jnp.arange(LANES) // HEAD_DIM
    gmat = jnp.where(grp[:, None] == grp[None, :], 1.0 / HEAD_DIM, 0.0).astype(BF16)
    pos = jnp.arange(B_CHUNK)
    ws = jnp.where((pos[:, None] // CHUNK) >= (pos[None, :] // CHUNK), w_s[0], 0.0).astype(BF16)
    bs_full = jnp.repeat(b_s[0].T, B_WIDTH // B_GROUPS, axis=1)
    slopes2 = jnp.exp2(-8.0 * jnp.arange(1, A_HEADS + 1, dtype=F32) / A_HEADS) * LOG2E
    rmax = (NORM_MARGIN * HEAD_DIM * jnp.max(jnp.abs(qg)) * jnp.max(jnp.abs(kg))).reshape(1)
    kp = jnp.arange(blk)
    tri = jnp.concatenate([jnp.where(kp[None, :] >= kp[:, None], -1.0, 0.0),
                           jnp.full((BF16_ROWS, blk), -1.0)], axis=0).astype(BF16)

    row_spec = lambda w: pl.BlockSpec((1, tm, w), lambda b, t: (b, t, 0))
    vt_out_spec = lambda w: pl.BlockSpec((1, tm // blk, w, blk), lambda b, t: (b, t, 0, 0))

    q0, k0, vt0, ga0, mb0 = pl.pallas_call(
        functools.partial(_even_in_kernel, tm=tm, blk=blk),
        grid=(bsz, nt),
        in_specs=[row_spec(d), _const_spec((1, d)), _const_spec(w_in_even.shape[1:]),
                  _const_spec((1, A_WIDTH)), _const_spec((1, A_WIDTH)), _const_spec((LANES, LANES)),
                  _const_spec((1, B_WIDTH)), _const_spec(ws.shape), _const_spec(bs_full.shape)],
        out_specs=[row_spec(A_WIDTH), row_spec(A_WIDTH), vt_out_spec(A_WIDTH), row_spec(A_WIDTH),
                   row_spec(B_WIDTH)],
        out_shape=[sds((bsz, seq, A_WIDTH), BF16), sds((bsz, seq, A_WIDTH), BF16),
                   sds((bsz, nb, A_WIDTH, blk), BF16), sds((bsz, seq, A_WIDTH), BF16),
                   sds((bsz, seq, B_WIDTH), BF16)],
        compiler_params=_params(2), name="even_in",
    )(x, norm_g[0:1], w_in_even[0].astype(BF16), qg, kg, gmat, sgu_norm_g[0:1], ws, bs_full)

    q_spec, kseq_spec, vtseq_spec = _attn_specs(nb, blk, qblk, DIFF_GROUPS)
    ma0 = pl.pallas_call(
        functools.partial(_diff_attn_kernel, blk=blk, qm=qm, groups=DIFF_GROUPS),
        grid=(bsz, A_HEADS // DIFF_GROUPS, nq),
        in_specs=[pl.BlockSpec(memory_space=pltpu.SMEM), pl.BlockSpec(memory_space=pltpu.SMEM),
                  q_spec, kseq_spec, vtseq_spec, q_spec, _const_spec(lam.shape[1:]), _const_spec((1, LANES))],
        out_specs=q_spec,
        out_shape=sds((bsz, seq, A_WIDTH), BF16),
        scratch_shapes=[pltpu.VMEM((DIFF_GROUPS, LANES, 2 * qblk), F32),
                        pltpu.VMEM((DIFF_GROUPS, blk, LANES), F32)],
        compiler_params=_params(3), name="diff_attn",
    )(slopes2, rmax, q0, k0.reshape(bsz, nb, blk, A_WIDTH), vt0, ga0, lam[0], subln_g[0:1])

    x1, q1, k1, vt1, gc1 = pl.pallas_call(
        functools.partial(_mid_kernel, tm=tm, blk=blk),
        grid=(bsz, nt),
        in_specs=[row_spec(d), row_spec(A_WIDTH), row_spec(B_WIDTH), _const_spec((d, d)),
                  _const_spec((1, d)), _const_spec(w_in_odd.shape[1:])],
        out_specs=[row_spec(d), row_spec(C_WIDTH), row_spec(C_WIDTH), vt_out_spec(C_WIDTH), row_spec(C_WIDTH)],
        out_shape=[sds((bsz, seq, d), F32), sds((bsz, seq, C_WIDTH), BF16), sds((bsz, seq, C_WIDTH), BF16),
                   sds((bsz, nb, C_WIDTH, blk), BF16), sds((bsz, seq, C_WIDTH), BF16)],
        compiler_params=_params(2), name="mid",
    )(x, ma0, mb0, w_out_even[0].astype(BF16), norm_g[1:2], w_in_odd[0].astype(BF16))

    q_spec, kseq_spec, vtseq_spec = _attn_specs(nb, blk, qblk, STICK_GROUPS)
    mc1 = pl.pallas_call(
        functools.partial(_stick_kernel, blk=blk, qm=qm, groups=STICK_GROUPS),
        grid=(bsz, C_WIDTH // LANES // STICK_GROUPS, nq),
        in_specs=[q_spec, kseq_spec, vtseq_spec, q_spec, _const_spec(tri.shape)],
        out_specs=q_spec,
        out_shape=sds((bsz, seq, C_WIDTH), BF16),
        scratch_shapes=[pltpu.VMEM((STICK_GROUPS, LANES, 2 * qblk), F32),
                        pltpu.VMEM((STICK_GROUPS, 1, 2 * qblk), F32),
                        pltpu.VMEM((STICK_GROUPS, 2 * qblk, LANES), BF16)],
        compiler_params=_params(3), name="stick_attn",
    )(q1, k1.reshape(bsz, nb, blk, C_WIDTH), vt1, gc1, tri)

    return pl.pallas_call(
        _out_kernel,
        grid=(bsz, nt),
        in_specs=[row_spec(d), row_spec(C_WIDTH), _const_spec((C_WIDTH, d))],
        out_specs=row_spec(d),
        out_shape=sds((bsz, seq, d), F32),
        compiler_params=_params(2), name="out_proj",
    )(x1, mc1, w_out_odd[0].astype(BF16))
```

```python
import functools
import math

import jax
import jax.numpy as jnp
from jax import lax
from jax.experimental import pallas as pl
from jax.experimental.pallas import tpu as pltpu

EPS = 1e-6
LOG2E = 1.4426950408889634
CHUNK = 64
HEAD_DIM = 64
LANES = 128
MXU_COLS = 256
BF16_ROWS = 16
A_HEADS = 4
A_WIDTH = 512
B_WIDTH = 512
B_CHUNK = 128
B_GROUPS = 4
C_WIDTH = 1024
LAMBDA_INIT_0 = 0.8 - 0.6 * math.exp(-0.3 * 0)
NEG_BIG = -1e30
EXP2_UNDERFLOW = -160.0
FROZEN_MAX_LIMIT = 40.0
NORM_MARGIN = 1.02
VMEM_LIMIT_BYTES = 56 * 1024 * 1024
DIFF_GROUPS = 4
STICK_GROUPS = 4
LOOKAHEAD = 2
SKIP = "skip"

F32 = jnp.float32
BF16 = jnp.bfloat16


def _tile_sizes(seq):
    blk = 256 if seq % 256 == 0 else 128
    tm = 512 if seq % 512 == 0 else blk
    qm = 2 if seq % (2 * blk) == 0 else 1
    return tm, blk, qm


def _dot(a, b):
    return jnp.dot(a, b, preferred_element_type=F32)


def _dot_nt(a, b):
    return lax.dot_general(a, b, (((1,), (1,)), ((), ())), preferred_element_type=F32)


def _silu(t):
    return t * (1.0 / (1.0 + jnp.exp(-t)))


def _rms(t, gain):
    ms = jnp.mean(t * t, axis=-1, keepdims=True)
    return t * lax.rsqrt(ms + EPS) * gain


def _split_halves(q):
    lane = lax.broadcasted_iota(jnp.int32, q.shape, 1)
    zero = jnp.zeros_like(q)
    return jnp.concatenate([jnp.where(lane < HEAD_DIM, q, zero), jnp.where(lane >= HEAD_DIM, q, zero)], axis=0)


def _tile(ct):
    return slice(ct * MXU_COLS, (ct + 1) * MXU_COLS)


def _tile_positions(blk, qblk, d, ct):
    ka = lax.broadcasted_iota(jnp.int32, (blk, MXU_COLS), 0) + d * blk
    ql = (lax.broadcasted_iota(jnp.int32, (blk, MXU_COLS), 1) + ct * MXU_COLS) & (qblk - 1)
    q_lo = (ct * MXU_COLS) % qblk
    return ka, ql, q_lo, q_lo + MXU_COLS - 1


def _pipeline(work, qk, process):
    raw = {k: qk(work[k]) for k in range(min(LOOKAHEAD, len(work)))}
    for k, item in enumerate(work):
        if k + LOOKAHEAD < len(work):
            raw[k + LOOKAHEAD] = qk(work[k + LOOKAHEAD])
        process(item, raw.pop(k))


def _even_in_kernel(x_ref, g_ref, w_ref, qg_ref, kg_ref, gmat_ref, sg_ref, ws_ref, bs_ref,
                    q_ref, k_ref, vt_ref, ga_ref, mb_ref, *, tm, blk):
    h = _rms(x_ref[0], g_ref[...]).astype(BF16)

    def proj(c0, width):
        return _dot(h, w_ref[:, c0:c0 + width])

    def head_norm(t, gain_ref):
        sq = t * t
        hi = sq.astype(BF16)
        lo = (sq - hi.astype(F32)).astype(BF16)
        slabs = [slice(s, s + LANES) for s in range(0, t.shape[1], LANES)]
        ms = jnp.concatenate([_dot(hi[:, s], gmat_ref[...]) + _dot(lo[:, s], gmat_ref[...]) for s in slabs], axis=1)
        return t * lax.rsqrt(ms + EPS) * gain_ref[...]

    q_ref[0] = head_norm(proj(0, A_WIDTH), qg_ref).astype(BF16)
    k_ref[0] = head_norm(proj(A_WIDTH, A_WIDTH), kg_ref).astype(BF16)
    va = proj(2 * A_WIDTH, A_WIDTH)
    for j in range(tm // blk):
        vt_ref[0, j] = va[j * blk:(j + 1) * blk, :].T.astype(BF16)
    ga_ref[0] = _silu(proj(3 * A_WIDTH, A_WIDTH)).astype(BF16)

    base = 4 * A_WIDTH
    u = jax.nn.gelu(proj(base, B_WIDTH))
    vn = _rms(jax.nn.gelu(proj(base + B_WIDTH, B_WIDTH)), sg_ref[...]).astype(BF16)
    sgate = _silu(proj(base + 2 * B_WIDTH, B_WIDTH))
    gd = B_WIDTH // B_GROUPS
    for c in range(tm // B_CHUNK):
        rows = slice(c * B_CHUNK, (c + 1) * B_CHUNK)
        for g in range(B_GROUPS):
            cols = slice(g * gd, (g + 1) * gd)
            mixed = _dot(ws_ref[g], vn[rows, cols]) + bs_ref[:, cols]
            mb_ref[0, rows, cols] = (u[rows, cols] * mixed * sgate[rows, cols]).astype(BF16)


def _diff_attn_kernel(sl_ref, rmax_ref, q_ref, k_ref, vt_ref, gate_ref, lam_ref, sub_ref, o_ref, acc_ref, kb_ref,
                      *, blk, qm, groups):
    hg = pl.program_id(1)
    i = pl.program_id(2)
    qblk = qm * blk
    n = 2 * qblk
    heads = range(groups)
    lanes = [slice(g * LANES, (g + 1) * LANES) for g in heads]
    slope2 = [sl_ref[hg * groups + g] for g in heads]
    qz = [_split_halves(q_ref[0, :, lanes[g]]) for g in heads]

    def update(hs, ms, ls, e, shift, kb, frozen=False):
        if frozen:
            p = {g: jnp.exp2(e[g] - (ms[g] - shift[g])) for g in hs}
            pv = {g: _dot(vt_ref[0, kb, lanes[g], :], p[g].astype(BF16)) for g in hs}
            for g in hs:
                acc_ref[g] += pv[g]
            l_new = {g: ls[g] + jnp.sum(p[g], axis=0, keepdims=True) for g in hs}
            return ms, tuple(l_new.get(g, ls[g]) for g in heads)
        m_new = {g: jnp.maximum(ms[g], jnp.max(e[g], axis=0, keepdims=True) + shift[g]) for g in hs}
        p = {g: jnp.exp2(e[g] - (m_new[g] - shift[g])) for g in hs}
        pv = {g: _dot(vt_ref[0, kb, lanes[g], :], p[g].astype(BF16)) for g in hs}
        alpha = {g: jnp.exp2(ms[g] - m_new[g]) for g in hs}
        for g in hs:
            acc_ref[g] = acc_ref[g] * alpha[g] + pv[g]
        l_new = {g: alpha[g] * ls[g] + jnp.sum(p[g], axis=0, keepdims=True) for g in hs}
        return tuple(m_new.get(g, ms[g]) for g in heads), tuple(l_new.get(g, ls[g]) for g in heads)

    krow = lax.broadcasted_iota(jnp.int32, (blk, LANES), 0).astype(F32)
    for g in heads:
        acc_ref[g] = jnp.zeros((LANES, n), F32)
        kb_ref[g] = slope2[g] * krow
    state = (tuple(jnp.full((1, n), NEG_BIG, F32) for _ in heads), tuple(jnp.zeros((1, n), F32) for _ in heads))

    def update_cols(ms, ls, e, kb, cols):
        w = cols[0].stop - cols[0].start
        take = lambda x: jnp.concatenate([x[:, c] for c in cols], axis=1)
        put = lambda full, part: jnp.concatenate(
            [full[:, :cols[0].start], part[:, :w], full[:, cols[0].stop:cols[1].start], part[:, w:]], axis=1)
        m_old = [take(ms[g]) for g in heads]
        m_new = [jnp.maximum(m_old[g], jnp.max(e[g], axis=0, keepdims=True)) for g in heads]
        p = [jnp.exp2(e[g] - m_new[g]) for g in heads]
        pv = [_dot(vt_ref[0, kb, lanes[g], :], p[g].astype(BF16)) for g in heads]
        alpha = [jnp.exp2(m_old[g] - m_new[g]) for g in heads]
        for g in heads:
            for half, c in enumerate(cols):
                part = slice(half * w, (half + 1) * w)
                acc_ref[g, :, c] = acc_ref[g, :, c] * alpha[g][:, part] + pv[g][:, part]
        l_new = [alpha[g] * take(ls[g]) + jnp.sum(p[g], axis=0, keepdims=True) for g in heads]
        return tuple(put(ms[g], m_new[g]) for g in heads), tuple(put(ls[g], l_new[g]) for g in heads)

    for d in range(qm):
        lo = d * blk
        w = qblk - lo
        cols = [slice(lo, qblk), slice(qblk + lo, n)]
        lane = lax.broadcasted_iota(jnp.int32, (blk, 2 * w), 1)
        ql = lo + jnp.where(lane < w, lane, lane - w)
        ka = lax.broadcasted_iota(jnp.int32, (blk, 2 * w), 0) + lo
        allowed = (ka // CHUNK) <= (ql // CHUNK)
        rel = (ql - jnp.abs(ka - ql)).astype(F32)
        kb = i * qm + d
        qd = [qz[g] if d == 0 else jnp.concatenate([qz[g][c] for c in cols], axis=0) for g in heads]
        raw = [_dot_nt(k_ref[0, kb, :, lanes[g]], qd[g]) for g in heads]
        e = [jnp.where(allowed, raw[g] + slope2[g] * rel, NEG_BIG) for g in heads]
        state = update(heads, *state, e, [0.0] * groups, kb) if d == 0 else update_cols(*state, e, kb, cols)

    total = i * qm
    rmax = rmax_ref[0]

    def update_two(hs, ms, ls, kbs):
        chains = [(g, t) for t in range(2) for g in hs]
        shift = {(g, t): slope2[g] * (kbs[t] * blk - i * qblk).astype(F32) for g, t in chains}
        raw = {(g, t): _dot_nt(k_ref[0, kbs[t], :, lanes[g]], qz[g]) for g, t in chains}
        kbias = {g: jnp.concatenate([kb_ref[g]] * (n // LANES), axis=1) for g in hs}
        p = {(g, t): jnp.exp2(raw[g, t] + kbias[g] - (ms[g] - shift[g, t])) for g, t in chains}
        pv = {(g, t): _dot(vt_ref[0, kbs[t], lanes[g], :], p[g, t].astype(BF16)) for g, t in chains}
        for g in hs:
            acc_ref[g] += pv[g, 0] + pv[g, 1]
        l_new = {g: ls[g] + jnp.sum(p[g, 0], axis=0, keepdims=True) + jnp.sum(p[g, 1], axis=0, keepdims=True)
                 for g in hs}
        return tuple(l_new.get(g, ls[g]) for g in heads)

    def walk(hs, watched, state, frozen, two=False):
        def reach(j, mmin):
            far = (j * blk).astype(F32)
            alive = [rmax - slope2[g] * far - mmin[g] > EXP2_UNDERFLOW for g in watched]
            return jnp.logical_and(j + int(two) < total, functools.reduce(jnp.logical_or, alive))

        def body(carry):
            j, ms, ls, mmin = carry
            if two:
                return j + 2, ms, update_two(hs, ms, ls, [total - 1 - j, total - 2 - j]), mmin
            kb = total - 1 - j
            off = (kb * blk - i * qblk).astype(F32)
            raw = {g: _dot_nt(k_ref[0, kb, :, lanes[g]], qz[g]) for g in hs}
            e = {g: raw[g] + jnp.concatenate([kb_ref[g]] * (n // LANES), axis=1) for g in hs}
            if not frozen:
                mmin = tuple(jnp.min(ms[g]) for g in heads)
            ms, ls = update(hs, ms, ls, e, {g: slope2[g] * off for g in hs}, kb, frozen)
            return j + 1, ms, ls, mmin

        j, ms, ls = state
        mmin = tuple(jnp.min(ms[g]) for g in heads)
        j, ms, ls, _ = lax.while_loop(lambda c: reach(c[0], c[3]), body, (j, ms, ls, mmin))
        return j, ms, ls

    hlist = list(heads)
    last = hlist[groups - 2:]
    phases = [(hlist[k:], [k], False) for k in range(groups - 2)] + [(last, last, True), (last, last, False)]

    def walks(frozen):
        def run():
            st = (jnp.int32(0),) + state
            for hs, watched, two in phases:
                if frozen or not two:
                    st = walk(hs, watched, st, frozen, two)
            return st[2]
        return run

    ls = lax.cond(rmax <= FROZEN_MAX_LIMIT, walks(True), walks(False))

    lam = lam_ref[...]
    lam_full = (jnp.exp(jnp.sum(lam[0:1] * lam[1:2], axis=1, keepdims=True))
                - jnp.exp(jnp.sum(lam[2:3] * lam[3:4], axis=1, keepdims=True)) + LAMBDA_INIT_0)
    for g in heads:
        acc = acc_ref[g] * (1.0 / ls[g])
        o = (acc[:, :qblk] - lam_full * acc[:, qblk:]).T
        o = _rms(o, sub_ref[...]) * (1.0 - LAMBDA_INIT_0)
        o_ref[0, :, lanes[g]] = (o * gate_ref[0, :, lanes[g]].astype(F32)).astype(BF16)


def _mid_kernel(x_ref, ma_ref, mb_ref, wo_ref, g_ref, wi_ref,
                x1_ref, q_ref, k_ref, vt_ref, gc_ref, *, tm, blk):
    x1 = x_ref[0] + _dot(ma_ref[0], wo_ref[0:A_WIDTH, :]) + _dot(mb_ref[0], wo_ref[A_WIDTH:, :])
    x1_ref[0] = x1
    h = _rms(x1, g_ref[...]).astype(BF16)

    def proj(c0):
        return _dot(h, wi_ref[:, c0:c0 + C_WIDTH])

    q_ref[0] = (proj(0) * (HEAD_DIM ** -0.5 * LOG2E)).astype(BF16)
    k_ref[0] = proj(C_WIDTH).astype(BF16)
    v = proj(2 * C_WIDTH)
    for j in range(tm // blk):
        vt_ref[0, j] = v[j * blk:(j + 1) * blk, :].T.astype(BF16)
    gc_ref[0] = _silu(proj(3 * C_WIDTH)).astype(BF16)


def _stick_kernel(q_ref, k_ref, vt_ref, gate_ref, tri_ref, o_ref, acc_ref, c_ref, qz_ref, *, blk, qm, groups):
    i = pl.program_id(2)
    qblk = qm * blk
    n = 2 * qblk
    pairs = range(groups)
    lanes = [slice(g * LANES, (g + 1) * LANES) for g in pairs]
    items = [(g, ct) for g in pairs for ct in range(n // MXU_COLS)]

    for g in pairs:
        qz_ref[g] = _split_halves(q_ref[0, :, lanes[g]])
        acc_ref[g] = jnp.zeros((LANES, n), F32)
        c_ref[g] = jnp.zeros((1, n), F32)

    def softplus2(z):
        return jnp.maximum(z, 0.0) + jnp.log(1.0 + jnp.exp2(-jnp.abs(z))) * LOG2E

    def step(blocks):
        work = [(kb, kinds[ct], g, ct) for kb, kinds in blocks for g, ct in items if kinds[ct] is not SKIP]

        def qk(item):
            kb, _, g, ct = item
            return _dot_nt(k_ref[0, kb, :, lanes[g]], qz_ref[g, _tile(ct), :])

        pending = []

        def finish():
            (kb, kind, g, ct), z, s = pending.pop()
            a = jnp.exp2(z + s[:blk])
            if kind is not None:
                a = jnp.where(kind, a, 0.0)
            pv = _dot(vt_ref[0, kb, lanes[g], :], a.astype(BF16))
            carry = c_ref[g, :, _tile(ct)]
            acc_ref[g, :, _tile(ct)] += pv * jnp.exp2(carry)
            c_ref[g, :, _tile(ct)] = carry + s[blk:blk + 1]

        def process(item, z):
            p = softplus2(z)
            if item[1] is not None:
                p = jnp.where(item[1], p, 0.0)
            s = _dot(tri_ref[...], p.astype(BF16))
            if pending:
                finish()
            pending.append((item, z, s))

        _pipeline(work, qk, process)
        finish()

    diagonal = []
    for d in reversed(range(qm)):
        kinds = {}
        for ct in range(n // MXU_COLS):
            ka, ql, q_lo, q_hi = _tile_positions(blk, qblk, d, ct)
            if d * blk >= q_hi:
                kinds[ct] = SKIP
            elif d * blk + blk - 1 < q_lo:
                kinds[ct] = None
            else:
                kinds[ct] = ka < ql
        diagonal.append((i * qm + d, kinds))
    step(diagonal)

    tiles = range(n // MXU_COLS)
    all_open = {ct: None for ct in tiles}
    first_only = {ct: (None if (ct * MXU_COLS) % qblk == 0 else SKIP) for ct in tiles}
    later = [_tile(ct) for ct in tiles if first_only[ct] is SKIP]

    def reach():
        alive = jnp.max(c_ref[...]) > EXP2_UNDERFLOW
        if not later:
            return alive, alive
        return alive, functools.reduce(jnp.logical_or, [jnp.max(c_ref[:, :, t]) > EXP2_UNDERFLOW for t in later])

    def body(carry):
        j, _, later_alive = carry
        kb = i * qm - 1 - j
        if later:
            lax.cond(later_alive, lambda: (step([(kb, all_open)]), 0)[1], lambda: (step([(kb, first_only)]), 0)[1])
        else:
            step([(kb, all_open)])
        return (j + 1,) + reach()

    lax.while_loop(lambda c: jnp.logical_and(c[0] < i * qm, c[1]), body, (jnp.int32(0),) + reach())

    for g in pairs:
        acc = acc_ref[g]
        o = jnp.concatenate([acc[:HEAD_DIM, :qblk], acc[HEAD_DIM:, qblk:]], axis=0).T
        o_ref[0, :, lanes[g]] = (o * gate_ref[0, :, lanes[g]].astype(F32)).astype(BF16)


def _out_kernel(x_ref, m_ref, w_ref, o_ref):
    o_ref[0] = x_ref[0] + _dot(m_ref[0], w_ref[...])


def _params(n_axes):
    return pltpu.CompilerParams(dimension_semantics=("arbitrary",) * n_axes,
                                vmem_limit_bytes=VMEM_LIMIT_BYTES)


def _const_spec(shape):
    nd = len(shape)
    return pl.BlockSpec(shape, lambda *_: (0,) * nd)


def _attn_specs(nb, blk, qblk, groups):
    w = groups * LANES
    qspec = pl.BlockSpec((1, qblk, w), lambda b, h, i: (b, i, h))
    kseq = pl.BlockSpec((1, nb, blk, w), lambda b, h, i: (b, 0, 0, h))
    vtseq = pl.BlockSpec((1, nb, w, blk), lambda b, h, i: (b, 0, h, 0))
    return qspec, kseq, vtseq


def kernel(x, norm_g, w_in_even, q_norm_g, k_norm_g, lam, subln_g, sgu_norm_g, w_s, b_s,
           w_out_even, w_in_odd, w_out_odd):
    bsz, seq, d = x.shape
    tm, blk, qm = _tile_sizes(seq)
    qblk = qm * blk
    nt, nb, nq = seq // tm, seq // blk, seq // qblk
    sds = jax.ShapeDtypeStruct

    reps = A_WIDTH // HEAD_DIM
    qg = (jnp.tile(q_norm_g[0], reps) * (HEAD_DIM ** -0.5 * LOG2E)).reshape(1, A_WIDTH)
    kg = jnp.tile(k_norm_g[0], reps).reshape(1, A_WIDTH)
    grp = jnp.arange(LANES) // HEAD_DIM
    gmat = jnp.where(grp[:, None] == grp[None, :], 1.0 / HEAD_DIM, 0.0).astype(BF16)
    pos = jnp.arange(B_CHUNK)
    ws = jnp.where((pos[:, None] // CHUNK) >= (pos[None, :] // CHUNK), w_s[0], 0.0).astype(BF16)
    bs_full = jnp.repeat(b_s[0].T, B_WIDTH // B_GROUPS, axis=1)
    slopes2 = jnp.exp2(-8.0 * jnp.arange(1, A_HEADS + 1, dtype=F32) / A_HEADS) * LOG2E
    rmax = (NORM_MARGIN * HEAD_DIM * jnp.max(jnp.abs(qg)) * jnp.max(jnp.abs(kg))).reshape(1)
    kp = jnp.arange(blk)
    tri = jnp.concatenate([jnp.where(kp[None, :] >= kp[:, None], -1.0, 0.0),
                           jnp.full((BF16_ROWS, blk), -1.0)], axis=0).astype(BF16)

    row_spec = lambda w: pl.BlockSpec((1, tm, w), lambda b, t: (b, t, 0))
    vt_out_spec = lambda w: pl.BlockSpec((1, tm // blk, w, blk), lambda b, t: (b, t, 0, 0))

    q0, k0, vt0, ga0, mb0 = pl.pallas_call(
        functools.partial(_even_in_kernel, tm=tm, blk=blk),
        grid=(bsz, nt),
        in_specs=[row_spec(d), _const_spec((1, d)), _const_spec(w_in_even.shape[1:]),
                  _const_spec((1, A_WIDTH)), _const_spec((1, A_WIDTH)), _const_spec((LANES, LANES)),
                  _const_spec((1, B_WIDTH)), _const_spec(ws.shape), _const_spec(bs_full.shape)],
        out_specs=[row_spec(A_WIDTH), row_spec(A_WIDTH), vt_out_spec(A_WIDTH), row_spec(A_WIDTH),
                   row_spec(B_WIDTH)],
        out_shape=[sds((bsz, seq, A_WIDTH), BF16), sds((bsz, seq, A_WIDTH), BF16),
                   sds((bsz, nb, A_WIDTH, blk), BF16), sds((bsz, seq, A_WIDTH), BF16),
                   sds((bsz, seq, B_WIDTH), BF16)],
        compiler_params=_params(2), name="even_in",
    )(x, norm_g[0:1], w_in_even[0].astype(BF16), qg, kg, gmat, sgu_norm_g[0:1], ws, bs_full)

    q_spec, kseq_spec, vtseq_spec = _attn_specs(nb, blk, qblk, DIFF_GROUPS)
    ma0 = pl.pallas_call(
        functools.partial(_diff_attn_kernel, blk=blk, qm=qm, groups=DIFF_GROUPS),
        grid=(bsz, A_HEADS // DIFF_GROUPS, nq),
        in_specs=[pl.BlockSpec(memory_space=pltpu.SMEM), pl.BlockSpec(memory_space=pltpu.SMEM),
                  q_spec, kseq_spec, vtseq_spec, q_spec, _const_spec(lam.shape[1:]), _const_spec((1, LANES))],
        out_specs=q_spec,
        out_shape=sds((bsz, seq, A_WIDTH), BF16),
        scratch_shapes=[pltpu.VMEM((DIFF_GROUPS, LANES, 2 * qblk), F32),
                        pltpu.VMEM((DIFF_GROUPS, blk, LANES), F32)],
        compiler_params=_params(3), name="diff_attn",
    )(slopes2, rmax, q0, k0.reshape(bsz, nb, blk, A_WIDTH), vt0, ga0, lam[0], subln_g[0:1])

    x1, q1, k1, vt1, gc1 = pl.pallas_call(
        functools.partial(_mid_kernel, tm=tm, blk=blk),
        grid=(bsz, nt),
        in_specs=[row_spec(d), row_spec(A_WIDTH), row_spec(B_WIDTH), _const_spec((d, d)),
                  _const_spec((1, d)), _const_spec(w_in_odd.shape[1:])],
        out_specs=[row_spec(d), row_spec(C_WIDTH), row_spec(C_WIDTH), vt_out_spec(C_WIDTH), row_spec(C_WIDTH)],
        out_shape=[sds((bsz, seq, d), F32), sds((bsz, seq, C_WIDTH), BF16), sds((bsz, seq, C_WIDTH), BF16),
                   sds((bsz, nb, C_WIDTH, blk), BF16), sds((bsz, seq, C_WIDTH), BF16)],
        compiler_params=_params(2), name="mid",
    )(x, ma0, mb0, w_out_even[0].astype(BF16), norm_g[1:2], w_in_odd[0].astype(BF16))

    q_spec, kseq_spec, vtseq_spec = _attn_specs(nb, blk, qblk, STICK_GROUPS)
    mc1 = pl.pallas_call(
        functools.partial(_stick_kernel, blk=blk, qm=qm, groups=STICK_GROUPS),
        grid=(bsz, C_WIDTH // LANES // STICK_GROUPS, nq),
        in_specs=[q_spec, kseq_spec, vtseq_spec, q_spec, _const_spec(tri.shape)],
        out_specs=q_spec,
        out_shape=sds((bsz, seq, C_WIDTH), BF16),
        scratch_shapes=[pltpu.VMEM((STICK_GROUPS, LANES, 2 * qblk), F32),
                        pltpu.VMEM((STICK_GROUPS, 1, 2 * qblk), F32),
                        pltpu.VMEM((STICK_GROUPS, 2 * qblk, LANES), BF16)],
        compiler_params=_params(3), name="stick_attn",
    )(q1, k1.reshape(bsz, nb, blk, C_WIDTH), vt1, gc1, tri)

    return pl.pallas_call(
        _out_kernel,
        grid=(bsz, nt),
        in_specs=[row_spec(d), row_spec(C_WIDTH), _const_spec((C_WIDTH, d))],
        out_specs=row_spec(d),
        out_shape=sds((bsz, seq, d), F32),
        compiler_params=_params(2), name="out_proj",
    )(x1, mc1, w_out_odd[0].astype(BF16))
```
